```python
import math
import jax, jax.numpy as jnp
from jax import lax
import numpy as np

D_MODEL = 1024
BATCH = 8
SEQ = 4096
DEPTH = 2
DEC_BATCH = 32
DEC_SEQ = 1
PAST_LEN = 16384
PAGE_SIZE = 128

HEAD_DIM = 64
RG_W = D_MODEL // 2
RG_BLOCKS = RG_W // HEAD_DIM
RG_BW = RG_W // RG_BLOCKS
RG_CONV = 4
RG_C = 8.0
MOBA_HEADS = (D_MODEL // 2) // HEAD_DIM
MOBA_W = MOBA_HEADS * HEAD_DIM
MOBA_BLOCK = 256
MOBA_TOPK = 3
DSA_HEADS = D_MODEL // HEAD_DIM
DSA_KV_HEADS = DSA_HEADS // 4
IDX_HEADS = 8
IDX_DIM = 64
DSA_TOPK = 256
D_FF = 3 * D_MODEL
FFN_CONV = 3
Q_BLOCK = 128
ROPE_THETA = 10000.0
EPS = 1e-6
IN0_SPLITS = (RG_W, 2 * RG_W, 2 * RG_W + MOBA_W, 2 * RG_W + 2 * MOBA_W)
IN0_W = 2 * RG_W + 3 * MOBA_W
DSA_Q_W = DSA_HEADS * HEAD_DIM
DSA_KV_W = DSA_KV_HEADS * HEAD_DIM
IN1_SPLITS = (DSA_Q_W, DSA_Q_W + DSA_KV_W, DSA_Q_W + 2 * DSA_KV_W,
              DSA_Q_W + 2 * DSA_KV_W + IDX_HEADS * IDX_DIM,
              DSA_Q_W + 2 * DSA_KV_W + IDX_HEADS * IDX_DIM + IDX_DIM)
IN1_W = IN1_SPLITS[-1] + IDX_HEADS

kernel_name = 'hybrid_rglru_moba_dsa_convffn_step'

F32 = jnp.float32


def rmsnorm(x, g):
    xf = x.astype(F32)
    return (xf * lax.rsqrt(jnp.mean(xf * xf, -1, keepdims=True) + EPS) * g.astype(F32)).astype(x.dtype)


def rope(x, pos):
    d = x.shape[-1]
    half = d // 2
    inv = ROPE_THETA ** (-jnp.arange(half, dtype=F32) * 2.0 / d)
    ang = pos.astype(F32)[:, None] * inv[None, :]
    cos = jnp.cos(ang)[:, None, :]
    sin = jnp.sin(ang)[:, None, :]
    x1 = x[..., :half].astype(F32)
    x2 = x[..., half:].astype(F32)
    return jnp.concatenate([x1 * cos - x2 * sin, x2 * cos + x1 * sin], -1).astype(x.dtype)


def causal_dwconv(prev, u, w, b):
    width = w.shape[0]
    T = u.shape[1]
    ext = jnp.concatenate([prev.astype(u.dtype), u], 1)
    out = ext[:, 0:T] * w[0]
    for j in range(1, width):
        out = out + ext[:, j:j + T] * w[j]
    return out + b, ext[:, ext.shape[1] - (width - 1):]


def rglru_branch(u, conv_prev, h_prev, conv_w, conv_b, wa, ba, wx, bx, lam):
    B, T, W = u.shape
    uc, conv_new = causal_dwconv(conv_prev, u, conv_w, conv_b)
    ucf = uc.astype(F32)
    ub = ucf.reshape(B, T, RG_BLOCKS, RG_BW)
    r = jax.nn.sigmoid(jnp.einsum('btnc,ncd->btnd', ub, wa.astype(F32)).reshape(B, T, W) + ba.astype(F32))
    i = jax.nn.sigmoid(jnp.einsum('btnc,ncd->btnd', ub, wx.astype(F32)).reshape(B, T, W) + bx.astype(F32))
    log_a = -RG_C * jax.nn.softplus(-lam.astype(F32)) * r
    a = jnp.exp(log_a)
    b_in = jnp.sqrt(-jnp.expm1(2.0 * log_a)) * i * ucf

    def step(h, ab):
        h = ab[0] * h + ab[1]
        return h, h

    h_last, hs = lax.scan(step, h_prev.astype(F32), (a.swapaxes(0, 1), b_in.swapaxes(0, 1)))
    return hs.swapaxes(0, 1).astype(u.dtype), h_last.astype(h_prev.dtype), conv_new


def moba_attention(q, k, v, q_pos):
    B, T, H, Dh = q.shape
    L = k.shape[1]
    nblk = -(-L // MOBA_BLOCK)
    pad = nblk * MOBA_BLOCK - L
    kp = jnp.pad(k, ((0, 0), (0, pad), (0, 0), (0, 0))).reshape(B, nblk, MOBA_BLOCK, H, Dh)
    vp = jnp.pad(v, ((0, 0), (0, pad), (0, 0), (0, 0))).reshape(B, nblk, MOBA_BLOCK, H, Dh)
    k_mean = jnp.mean(kp.astype(F32), axis=2)
    kb = kp.transpose(0, 3, 1, 2, 4)
    vb = vp.transpose(0, 3, 1, 2, 4)
    qb = math.gcd(T, Q_BLOCK)
    nqb = T // qb
    qr = q.reshape(B, nqb, qb, H, Dh)
    pr = q_pos.reshape(nqb, qb)
    n_top = min(MOBA_TOPK, nblk)
    head_idx = jnp.arange(H)[None, :, None]
    blk_ids = jnp.arange(nblk)
    in_blk = jnp.arange(MOBA_BLOCK)
    scale = HEAD_DIM ** -0.5

    def attend(bj):
        b = bj[0]
        j = bj[1]
        qq = qr[b, j].astype(F32)
        pp = pr[j]
        cur = pp // MOBA_BLOCK
        gate = jnp.einsum('qhd,nhd->qhn', qq, k_mean[b])
        past = blk_ids[None, :] < cur[:, None]
        gate = jnp.where(past[:, None, :], gate, -jnp.inf)
        g_val, g_idx = lax.top_k(gate, n_top)
        sel = jnp.concatenate([g_idx, jnp.broadcast_to(cur[:, None, None], (qb, H, 1)).astype(g_idx.dtype)], -1)
        blk_ok = jnp.concatenate([jnp.isfinite(g_val), jnp.ones((qb, H, 1), bool)], -1)
        k_sel = kb[b, head_idx, sel].astype(F32)
        v_sel = vb[b, head_idx, sel].astype(F32)
        key_pos = sel[..., None] * MOBA_BLOCK + in_blk
        ok = blk_ok[..., None] & (key_pos <= pp[:, None, None, None])
        s = jnp.einsum('qhd,qhnkd->qhnk', qq, k_sel) * scale
        s = jnp.where(ok, s, -jnp.inf).reshape(qb, H, -1)
        p = jax.nn.softmax(s, -1).reshape(qb, H, n_top + 1, MOBA_BLOCK)
        return jnp.einsum('qhnk,qhnkd->qhd', p, v_sel).astype(q.dtype)

    bj = jnp.stack([jnp.repeat(jnp.arange(B), nqb), jnp.tile(jnp.arange(nqb), B)], -1)
    out = lax.map(attend, bj)
    return out.reshape(B, T, H, Dh)


def dsa_attention(q, k, v, q_idx, k_idx, w_idx, q_pos):
    B, T, H, Dh = q.shape
    L = k.shape[1]
    G = H // DSA_KV_HEADS
    n_sel = min(DSA_TOPK, L // 4)
    qb = math.gcd(T, Q_BLOCK)
    nqb = T // qb
    qr = q.reshape(B, nqb, qb, H, Dh)
    qir = q_idx.reshape(B, nqb, qb, IDX_HEADS, IDX_DIM)
    wr = w_idx.reshape(B, nqb, qb, IDX_HEADS)
    pr = q_pos.reshape(nqb, qb)
    key_pos = jnp.arange(L)
    kif = k_idx.astype(F32)
    gather_rows = jax.vmap(lambda rows, ids: rows[ids])

    def attend(j):
        pp = pr[j]
        qi = qir[:, j].astype(F32)
        rel = jax.nn.relu(jnp.einsum('bqhd,bsd->bqhs', qi, kif) * IDX_DIM ** -0.5)
        score = jnp.einsum('bqh,bqhs->bqs', wr[:, j].astype(F32) * IDX_HEADS ** -0.5, rel)
        score = jnp.where(key_pos[None, None, :] <= pp[None, :, None], score, -jnp.inf)
        _, idx = lax.top_k(score, n_sel)
        ok = idx <= pp[None, :, None]
        k_sel = gather_rows(k, idx).astype(F32)
        v_sel = gather_rows(v, idx).astype(F32)
        qq = qr[:, j].astype(F32).reshape(B, qb, DSA_KV_HEADS, G, Dh)
        s = jnp.einsum('bqngd,bqknd->bqngk', qq, k_sel) * HEAD_DIM ** -0.5
        s = jnp.where(ok[:, :, None, None, :], s, -jnp.inf)
        p = jax.nn.softmax(s, -1)
        o = jnp.einsum('bqngk,bqknd->bqngd', p, v_sel)
        return o.reshape(B, qb, H, Dh).astype(q.dtype)

    out = lax.map(attend, jnp.arange(nqb))
    return out.transpose(1, 0, 2, 3, 4).reshape(B, T, H, Dh)


def mixer_ab(xn, pos, k_past, v_past, h_prev, conv_prev, w_in, rg_conv_w, rg_conv_b,
             rg_gate_a_w, rg_gate_a_b, rg_gate_x_w, rg_gate_x_b, rg_lambda, w_out):
    B, T, _ = xn.shape
    proj = xn @ w_in
    u, g, q, k, v = jnp.split(proj, IN0_SPLITS, axis=-1)
    y_rg, h_new, conv_new = rglru_branch(u, conv_prev, h_prev, rg_conv_w, rg_conv_b,
                                         rg_gate_a_w, rg_gate_a_b, rg_gate_x_w, rg_gate_x_b, rg_lambda)
    y_rg = y_rg * jax.nn.gelu(g)
    q = rope(q.reshape(B, T, MOBA_HEADS, HEAD_DIM), pos)
    k = rope(k.reshape(B, T, MOBA_HEADS, HEAD_DIM), pos)
    v = v.reshape(B, T, MOBA_HEADS, HEAD_DIM)
    o = moba_attention(q, jnp.concatenate([k_past, k], 1), jnp.concatenate([v_past, v], 1), pos)
    out = jnp.concatenate([y_rg, o.reshape(B, T, MOBA_W)], -1) @ w_out
    return out, k, v, h_new, conv_new


def mixer_c(xn, pos, k_past, v_past, kidx_past, w_in, w_out):
    B, T, _ = xn.shape
    proj = xn @ w_in
    q, k, v, qi, ki, wi = jnp.split(proj, IN1_SPLITS, axis=-1)
    q = rope(q.reshape(B, T, DSA_HEADS, HEAD_DIM), pos)
    k = rope(k.reshape(B, T, DSA_KV_HEADS, HEAD_DIM), pos)
    v = v.reshape(B, T, DSA_KV_HEADS, HEAD_DIM)
    qi = rope(qi.reshape(B, T, IDX_HEADS, IDX_DIM), pos)
    ki = rope(ki[:, :, None, :], pos)[:, :, 0]
    o = dsa_attention(q, jnp.concatenate([k_past, k], 1), jnp.concatenate([v_past, v], 1),
                      qi, jnp.concatenate([kidx_past, ki], 1), wi, pos)
    return o.reshape(B, T, DSA_Q_W) @ w_out, k, v, ki


def conv_ffn(xn, prev, w_gate, w_up, conv_w, conv_b, w_down):
    g = xn @ w_gate
    u = xn @ w_up
    gc, new = causal_dwconv(prev, g, conv_w, conv_b)
    return (jax.nn.gelu(gc) * u) @ w_down, new


def run_group(x, pos, k0_past, v0_past, h0, conv0, k1_past, v1_past, kidx1_past, ffn_prev,
              norm_mix, norm_ffn, norm_final, w_in0, rg_conv_w, rg_conv_b, rg_gate_a_w, rg_gate_a_b,
              rg_gate_x_w, rg_gate_x_b, rg_lambda, w_out0, w_in1, w_out1,
              ffn_w_gate, ffn_w_up, ffn_conv_w, ffn_conv_b, ffn_w_down):
    ffn_new = []
    for layer in range(DEPTH):
        xn = rmsnorm(x, norm_mix[layer])
        if layer % 2 == 0:
            mix, k0n, v0n, h0n, conv0n = mixer_ab(xn, pos, k0_past, v0_past, h0, conv0, w_in0, rg_conv_w,
                                                  rg_conv_b, rg_gate_a_w, rg_gate_a_b, rg_gate_x_w,
                                                  rg_gate_x_b, rg_lambda, w_out0)
        else:
            mix, k1n, v1n, kidx1n = mixer_c(xn, pos, k1_past, v1_past, kidx1_past, w_in1, w_out1)
        x = x + mix
        f, fs = conv_ffn(rmsnorm(x, norm_ffn[layer]), ffn_prev[layer], ffn_w_gate[layer], ffn_w_up[layer],
                         ffn_conv_w[layer], ffn_conv_b[layer], ffn_w_down[layer])
        x = x + f
        ffn_new.append(fs)
    return rmsnorm(x, norm_final), k0n, v0n, h0n, conv0n, k1n, v1n, kidx1n, jnp.stack(ffn_new)


def setup_inputs(seed: int = 0) -> dict:
    key = jax.random.key(seed)
    ks = iter(jax.random.split(key, 40))
    n_pages = PAST_LEN // PAGE_SIZE
    used = DEC_BATCH * n_pages
    n_pool = used + max(1, used // 4)

    def nrm(shape, scale=1.0):
        return jax.random.normal(next(ks), shape, F32) * scale

    a0 = jax.random.uniform(next(ks), (RG_W,), F32, minval=0.9, maxval=0.999)
    s0 = a0 ** (1.0 / RG_C)
    rg_lambda = jnp.log(s0) - jnp.log1p(-s0)
    page_table = jax.random.permutation(next(ks), n_pool)[:used].reshape(DEC_BATCH, n_pages).astype(jnp.int32)
    return {
        'x_prompt': nrm((BATCH, SEQ, D_MODEL)),
        'x_sample': nrm((DEC_BATCH, DEC_SEQ, D_MODEL)),
        'cache_k0': nrm((n_pool, PAGE_SIZE, MOBA_HEADS, HEAD_DIM)),
        'cache_v0': nrm((n_pool, PAGE_SIZE, MOBA_HEADS, HEAD_DIM)),
        'state_h0': nrm((DEC_BATCH, RG_W), 0.5),
        'state_conv0': nrm((DEC_BATCH, RG_CONV - 1, RG_W)),
        'cache_k1': nrm((n_pool, PAGE_SIZE, DSA_KV_HEADS, HEAD_DIM)),
        'cache_v1': nrm((n_pool, PAGE_SIZE, DSA_KV_HEADS, HEAD_DIM)),
        'cache_kidx1': nrm((n_pool, PAGE_SIZE, IDX_DIM)),
        'state_ffn': nrm((DEPTH, DEC_BATCH, FFN_CONV - 1, D_FF)),
        'page_table': page_table,
        'norm_mix': 1.0 + nrm((DEPTH, D_MODEL), 0.05),
        'norm_ffn': 1.0 + nrm((DEPTH, D_MODEL), 0.05),
        'norm_final': 1.0 + nrm((D_MODEL,), 0.05),
        'w_in0': nrm((D_MODEL, IN0_W), D_MODEL ** -0.5),
        'rg_conv_w': nrm((RG_CONV, RG_W), RG_CONV ** -0.5),
        'rg_conv_b': nrm((RG_W,), 0.01),
        'rg_gate_a_w': nrm((RG_BLOCKS, RG_BW, RG_BW), RG_BW ** -0.5),
        'rg_gate_a_b': nrm((RG_W,), 0.01),
        'rg_gate_x_w': nrm((RG_BLOCKS, RG_BW, RG_BW), RG_BW ** -0.5),
        'rg_gate_x_b': nrm((RG_W,), 0.01),
        'rg_lambda': rg_lambda,
        'w_out0': nrm((RG_W + MOBA_W, D_MODEL), (RG_W + MOBA_W) ** -0.5),
        'w_in1': nrm((D_MODEL, IN1_W), D_MODEL ** -0.5),
        'w_out1': nrm((DSA_Q_W, D_MODEL), DSA_Q_W ** -0.5),
        'ffn_w_gate': nrm((DEPTH, D_MODEL, D_FF), D_MODEL ** -0.5),
        'ffn_w_up': nrm((DEPTH, D_MODEL, D_FF), D_MODEL ** -0.5),
        'ffn_conv_w': nrm((DEPTH, FFN_CONV, D_FF), FFN_CONV ** -0.5),
        'ffn_conv_b': nrm((DEPTH, D_FF), 0.01),
        'ffn_w_down': nrm((DEPTH, D_FF, D_MODEL), D_FF ** -0.5),
    }


def reference(x_prompt, x_sample, cache_k0, cache_v0, state_h0, state_conv0, cache_k1, cache_v1,
              cache_kidx1, state_ffn, page_table, norm_mix, norm_ffn, norm_final, w_in0, rg_conv_w,
              rg_conv_b, rg_gate_a_w, rg_gate_a_b, rg_gate_x_w, rg_gate_x_b, rg_lambda, w_out0, w_in1,
              w_out1, ffn_w_gate, ffn_w_up, ffn_conv_w, ffn_conv_b, ffn_w_down):
    dt = x_prompt.dtype
    bp = x_prompt.shape[0]
    pos_p = jnp.arange(x_prompt.shape[1], dtype=jnp.int32)
    (y_p, k0_p, v0_p, h0_p, conv0_p, k1_p, v1_p, kidx1_p, ffn_p) = run_group(
        x_prompt, pos_p,
        jnp.zeros((bp, 0, MOBA_HEADS, HEAD_DIM), dt), jnp.zeros((bp, 0, MOBA_HEADS, HEAD_DIM), dt),
        jnp.zeros((bp, RG_W), dt), jnp.zeros((bp, RG_CONV - 1, RG_W), dt),
        jnp.zeros((bp, 0, DSA_KV_HEADS, HEAD_DIM), dt), jnp.zeros((bp, 0, DSA_KV_HEADS, HEAD_DIM), dt),
        jnp.zeros((bp, 0, IDX_DIM), dt), jnp.zeros((DEPTH, bp, FFN_CONV - 1, D_FF), dt),
        norm_mix, norm_ffn, norm_final, w_in0, rg_conv_w, rg_conv_b, rg_gate_a_w, rg_gate_a_b,
        rg_gate_x_w, rg_gate_x_b, rg_lambda, w_out0, w_in1, w_out1,
        ffn_w_gate, ffn_w_up, ffn_conv_w, ffn_conv_b, ffn_w_down)
    db, n_pages = page_table.shape

    def gather_pages(cache):
        rows = cache[page_table]
        return rows.reshape((db, n_pages * PAGE_SIZE) + cache.shape[2:])

    pos_s = n_pages * PAGE_SIZE + jnp.arange(x_sample.shape[1], dtype=jnp.int32)
    (y_s, k0_s, v0_s, h0_s, conv0_s, k1_s, v1_s, kidx1_s, ffn_s) = run_group(
        x_sample, pos_s, gather_pages(cache_k0), gather_pages(cache_v0), state_h0, state_conv0,
        gather_pages(cache_k1), gather_pages(cache_v1), gather_pages(cache_kidx1), state_ffn,
        norm_mix, norm_ffn, norm_final, w_in0, rg_conv_w, rg_conv_b, rg_gate_a_w, rg_gate_a_b,
        rg_gate_x_w, rg_gate_x_b, rg_lambda, w_out0, w_in1, w_out1,
        ffn_w_gate, ffn_w_up, ffn_conv_w, ffn_conv_b, ffn_w_down)
    return (y_p, y_s, k0_p, v0_p, h0_p, conv0_p, k1_p, v1_p, kidx1_p, ffn_p,
            k0_s, v0_s, h0_s, conv0_s, k1_s, v1_s, kidx1_s, ffn_s)
```

```python
import functools

import jax
import jax.numpy as jnp
from jax import lax
from jax.experimental import pallas as pl
from jax.experimental.pallas import tpu as pltpu

F32 = jnp.float32
BF16 = jnp.bfloat16
I32 = jnp.int32

HEAD_DIM = 64
PAGE_SIZE = 128
RG_C = 8.0
MOBA_BLOCK = 256
MOBA_TOPK = 3
DSA_KV_HEADS = 4
IDX_HEADS = 8
IDX_DIM = 64
DSA_TOPK = 256
ROPE_THETA = 10000.0
EPS = 1e-6

LANES = 128
SUBLANES = 8
VMEM_LIMIT = 56 * 1024 * 1024
NEG = -1e30
INT_MIN = -(2 ** 31)

_HI = lax.Precision.HIGHEST


def _cparams(*sem):
    return pltpu.CompilerParams(dimension_semantics=sem, vmem_limit_bytes=VMEM_LIMIT)


def _const_spec(shape):
    nd = len(shape)
    return pl.BlockSpec(shape, lambda *_: (0,) * nd, pipeline_mode=pl.Buffered(1))


def _gelu(x):
    return x * (0.5 * (1.0 + jnp.tanh(0.7978845608028654 * (x + 0.044715 * (x * x * x)))))


def _sigmoid(x):
    return 1.0 / (1.0 + jnp.exp(-x))


def _rms(x, g):
    return x * lax.rsqrt(jnp.mean(x * x, axis=-1, keepdims=True) + EPS) * g


def _dot(a, b):
    return jnp.dot(a, b, preferred_element_type=F32)


def _dot_t(a, b, precision=None):
    return lax.dot_general(a, b, (((1,), (1,)), ((), ())), precision=precision,
                           preferred_element_type=F32)


def _rope_group(x, cos, sin_signed):
    lane = lax.broadcasted_iota(I32, x.shape, 1)
    first_half = (lane % HEAD_DIM) < (HEAD_DIM // 2)
    partner = jnp.where(first_half, pltpu.roll(x, LANES - HEAD_DIM // 2, 1), pltpu.roll(x, HEAD_DIM // 2, 1))
    return x * cos + partner * sin_signed


def _shift_rows(x, prev, j):
    r = pltpu.roll(x, j, 0)
    row = lax.broadcasted_iota(I32, (SUBLANES, x.shape[1]), 0)
    head = jnp.where(row < j, pltpu.roll(prev, j, 0), r[0:SUBLANES])
    if x.shape[0] == SUBLANES:
        return head
    return jnp.concatenate([head, r[SUBLANES:]], axis=0)


def _sortable(x):
    b = pltpu.bitcast(x, I32)
    return b ^ ((b >> 31) & 0x7FFFFFFF)


def _norm_proj_kernel(x_ref, g_ref, w_ref, cos_ref, sin_ref, cosk_ref, sink_ref, *out_refs, segs):
    xb = _rms(x_ref[...], g_ref[...]).astype(BF16)
    oi = 0
    for c0, width, rope, scale, outs in segs:
        for g0 in range(0, width, LANES):
            y = _dot(xb, w_ref[:, c0 + g0:c0 + g0 + LANES])
            if rope == "heads":
                y = _rope_group(y, cos_ref[...], sin_ref[...])
            elif rope == "half":
                y = _rope_group(y, cosk_ref[...], sink_ref[...])
            if scale != 1.0:
                y = y * scale
            for k, (dt, lanes) in enumerate(outs):
                if lanes == LANES:
                    out_refs[oi + k][:, g0:g0 + LANES] = y.astype(dt)
                else:
                    out_refs[oi + k][...] = y[:, :lanes].astype(dt)
        oi += len(outs)


def _norm_proj(x2d, gamma, w_bf16, tables, segs, tm, n_tab):
    n, d = x2d.shape
    out_shape, out_specs = [], []
    for c0, width, rope, scale, outs in segs:
        for dt, lanes in outs:
            wd = width if lanes == LANES else lanes
            out_shape.append(jax.ShapeDtypeStruct((n, wd), dt))
            out_specs.append(pl.BlockSpec((tm, wd), lambda i: (i, 0)))
    tab_spec = pl.BlockSpec((tm, LANES), lambda i: (i % n_tab, 0))
    return pl.pallas_call(
        functools.partial(_norm_proj_kernel, segs=segs),
        grid=(n // tm,),
        in_specs=[pl.BlockSpec((tm, d), lambda i: (i, 0)), _const_spec((1, d)), _const_spec(w_bf16.shape),
                  tab_spec, tab_spec, tab_spec, tab_spec],
        out_specs=out_specs, out_shape=out_shape,
        compiler_params=_cparams("parallel"), name="norm_proj",
    )(x2d, gamma.reshape(1, d), w_bf16, *tables)


def _rope_tables(pos, rows):
    half = HEAD_DIM // 2
    inv = ROPE_THETA ** (-jnp.arange(half, dtype=F32) * 2.0 / HEAD_DIM)
    ang = pos.astype(F32)[:, None] * inv[None, :]
    cos, sin = jnp.cos(ang), jnp.sin(ang)
    one, zero = jnp.ones_like(cos), jnp.zeros_like(sin)
    tabs = [jnp.concatenate([cos, cos, cos, cos], -1), jnp.concatenate([-sin, sin, -sin, sin], -1),
            jnp.concatenate([cos, cos, one, one], -1), jnp.concatenate([-sin, sin, zero, zero], -1)]
    return [jnp.broadcast_to(t, (rows, LANES)) for t in tabs]


def _rg_gate_math(uc, wa_ref, wx_ref, ba_ref, bx_ref, lam_ref):
    ub = uc.astype(BF16)
    r = _sigmoid(_dot(ub, wa_ref[...]) + ba_ref[...])
    i = _sigmoid(_dot(ub, wx_ref[...]) + bx_ref[...])
    nl = -lam_ref[...]
    softplus = jnp.maximum(nl, 0.0) + jnp.log1p(jnp.exp(-jnp.abs(nl)))
    log_a = (-RG_C * softplus) * r
    a = jnp.exp(log_a)
    th = jnp.tanh(log_a)
    b = jnp.sqrt(-2.0 * th / (1.0 - th)) * i * uc
    return a, b


def _rglru_seq_kernel(u_ref, g_ref, cw_ref, cb_ref, wa_ref, wx_ref, ba_ref, bx_ref, lam_ref,
                      y_ref, hl_ref, a_s, b_s, tail_s, h_s):
    nb, tc, w = u_ref.shape
    nslab = w // LANES

    @pl.when(pl.program_id(0) == 0)
    def _():
        tail_s[...] = jnp.zeros_like(tail_s)
        h_s[...] = jnp.zeros_like(h_s)

    for b in range(nb):
        u = u_ref[b]
        prev = tail_s[b]
        uc = (_shift_rows(u, prev, 3) * cw_ref[0:1, :] + _shift_rows(u, prev, 2) * cw_ref[1:2, :]
              + _shift_rows(u, prev, 1) * cw_ref[2:3, :] + u * cw_ref[3:4, :] + cb_ref[...])
        tail_s[b] = u[tc - SUBLANES:tc]
        a, bi = _rg_gate_math(uc, wa_ref, wx_ref, ba_ref, bx_ref, lam_ref)
        for l in range(nslab):
            a_s[l, pl.ds(b, tc, stride=nb), :] = a[:, l * LANES:(l + 1) * LANES]
            b_s[l, pl.ds(b, tc, stride=nb), :] = bi[:, l * LANES:(l + 1) * LANES]

    def step(t, hs):
        r0 = pl.multiple_of(t * nb, nb)
        new = []
        for l in range(nslab):
            h = a_s[l, pl.ds(r0, nb), :] * hs[l] + b_s[l, pl.ds(r0, nb), :]
            b_s[l, pl.ds(r0, nb), :] = h
            new.append(h)
        return tuple(new)

    hs = lax.fori_loop(0, tc, step, tuple(h_s[l] for l in range(nslab)), unroll=8)
    for l in range(nslab):
        h_s[l] = hs[l]
        hl_ref[:, l * LANES:(l + 1) * LANES] = hs[l]

    for b in range(nb):
        gate = _gelu(g_ref[b])
        for l in range(nslab):
            hb = b_s[l, pl.ds(b, tc, stride=nb), :]
            y_ref[b, :, l * LANES:(l + 1) * LANES] = (hb * gate[:, l * LANES:(l + 1) * LANES]).astype(y_ref.dtype)


def _rglru_seq(ug, cw, cb, wa_bd, wx_bd, ba, bx, lam, tc):
    nb, t, w2 = ug.shape
    w = w2 // 2
    assert nb == SUBLANES and t % tc == 0 and w % LANES == 0
    row = lambda a: a.reshape(1, w)
    return pl.pallas_call(
        _rglru_seq_kernel,
        grid=(t // tc,),
        in_specs=[pl.BlockSpec((nb, tc, w), lambda i: (0, i, 0)), pl.BlockSpec((nb, tc, w), lambda i: (0, i, 1)),
                  _const_spec(cw.shape), _const_spec((1, w)), _const_spec((w, w)), _const_spec((w, w)),
                  _const_spec((1, w)), _const_spec((1, w)), _const_spec((1, w))],
        out_specs=[pl.BlockSpec((nb, tc, w), lambda i: (0, i, 0)), pl.BlockSpec((nb, w), lambda i: (0, 0))],
        out_shape=[jax.ShapeDtypeStruct((nb, t, w), BF16), jax.ShapeDtypeStruct((nb, w), F32)],
        scratch_shapes=[pltpu.VMEM((w // LANES, tc * nb, LANES), F32), pltpu.VMEM((w // LANES, tc * nb, LANES), F32),
                        pltpu.VMEM((nb, SUBLANES, w), F32), pltpu.VMEM((w // LANES, nb, LANES), F32)],
        compiler_params=_cparams("arbitrary"), name="rglru_seq",
    )(ug, ug, cw, row(cb), wa_bd, wx_bd, row(ba), row(bx), row(lam))


def _rglru_step_kernel(u_ref, g_ref, cp_ref, h_ref, cw_ref, cb_ref, wa_ref, wx_ref, ba_ref, bx_ref, lam_ref,
                       y_ref, hn_ref):
    w = u_ref.shape[1]
    u = u_ref[...]
    uc = (cp_ref[:, 0:w] * cw_ref[0:1, :] + cp_ref[:, w:2 * w] * cw_ref[1:2, :]
          + cp_ref[:, 2 * w:3 * w] * cw_ref[2:3, :] + u * cw_ref[3:4, :] + cb_ref[...])
    a, bi = _rg_gate_math(uc, wa_ref, wx_ref, ba_ref, bx_ref, lam_ref)
    h = a * h_ref[...] + bi
    hn_ref[...] = h
    y_ref[...] = (h * _gelu(g_ref[...])).astype(y_ref.dtype)


def _rglru_step(ug, conv_prev, h_prev, cw, cb, wa_bd, wx_bd, ba, bx, lam):
    nb, w2 = ug.shape
    w = w2 // 2
    row = lambda a: a.reshape(1, w)
    full = lambda shape: pl.BlockSpec(shape, lambda i: (0,) * len(shape))
    return pl.pallas_call(
        _rglru_step_kernel,
        grid=(1,),
        in_specs=[pl.BlockSpec((nb, w), lambda i: (0, 0)), pl.BlockSpec((nb, w), lambda i: (0, 1)),
                  full((nb, 3 * w)), full((nb, w)), full(cw.shape), full((1, w)), full((w, w)), full((w, w)),
                  full((1, w)), full((1, w)), full((1, w))],
        out_specs=[full((nb, w)), full((nb, w))],
        out_shape=[jax.ShapeDtypeStruct((nb, w), BF16), jax.ShapeDtypeStruct((nb, w), F32)],
        compiler_params=_cparams("arbitrary"), name="rglru_step",
    )(ug, ug, conv_prev.reshape(nb, 3 * w), h_prev, cw, row(cb), wa_bd, wx_bd, row(ba), row(bx), row(lam))


def _moba_seq_kernel(q_ref, k_ref, v_ref, o_ref, km_s, bias_s, qb_s, m_s, l_s, acc_s, *, nblk):
    i = pl.program_id(2)
    blk = MOBA_BLOCK

    @pl.when(i == 0)
    def _():
        for n in range(nblk):
            km_s[n:n + 1, :] = jnp.mean(k_ref[0, n * blk:(n + 1) * blk, :].astype(F32), axis=0, keepdims=True)

    q = q_ref[0]
    lane = lax.broadcasted_iota(I32, (1, LANES), 1)
    blk_id = lax.broadcasted_iota(I32, (1, nblk), 1)
    row = lax.broadcasted_iota(I32, (blk, 1), 0)
    col = lax.broadcasted_iota(I32, (1, blk), 1)
    d0 = pl.multiple_of(i * blk, blk)
    kd = k_ref[0, pl.ds(d0, blk), :]
    vd = v_ref[0, pl.ds(d0, blk), :]

    for hh in range(2):
        qh = jnp.where((lane // HEAD_DIM) == hh, q, 0.0)
        gate = _dot_t(qh, km_s[...], precision=_HI)
        gate = jnp.where(blk_id < i, gate, -jnp.inf)
        rank = jnp.zeros(gate.shape, I32)
        for m in range(nblk):
            gm = gate[:, m:m + 1]
            rank = rank + jnp.where(gm > gate, 1, jnp.where(gm == gate, (m < blk_id).astype(I32), 0))
        bias = jnp.where(jnp.where(blk_id < i, rank, MOBA_TOPK) < MOBA_TOPK, 0.0, NEG)
        for n in range(nblk - 1):
            @pl.when(n < i)
            def _(n=n, bias=bias, hh=hh):
                bias_s[hh, n] = jnp.broadcast_to(bias[:, n:n + 1], (blk, LANES))
        qb = (qh * (HEAD_DIM ** -0.5)).astype(BF16)
        qb_s[hh] = qb
        s = jnp.where(col <= row, _dot_t(qb, kd), NEG)
        m = jnp.max(s, axis=1, keepdims=True)
        p = jnp.exp(s - m)
        m_s[hh] = m
        l_s[hh] = jnp.sum(p, axis=1, keepdims=True)
        acc_s[hh] = _dot(p.astype(BF16), vd)

    def past(n, carry):
        n0 = pl.multiple_of(n * blk, blk)
        kt = k_ref[0, pl.ds(n0, blk), :]
        vt = v_ref[0, pl.ds(n0, blk), :]
        for hh in range(2):
            bt = bias_s[hh, n]
            s = _dot_t(qb_s[hh], kt) + jnp.concatenate([bt] * (blk // LANES), axis=1)
            m_old = m_s[hh]
            m_new = jnp.maximum(m_old, jnp.max(s, axis=1, keepdims=True))
            alpha = jnp.exp(m_old - m_new)
            p = jnp.exp(s - m_new)
            l_s[hh] = alpha * l_s[hh] + jnp.sum(p, axis=1, keepdims=True)
            acc_s[hh] = alpha * acc_s[hh] + _dot(p.astype(BF16), vt)
            m_s[hh] = m_new
        return carry

    lax.fori_loop(0, i, past, 0)
    o = jnp.where((lane // HEAD_DIM) == 0, acc_s[0] / l_s[0], acc_s[1] / l_s[1])
    o_ref[0] = o.astype(o_ref.dtype)


def _moba_seq(q, kb, vb):
    nb, t, hw = q.shape
    blk = MOBA_BLOCK
    assert t % blk == 0 and hw % LANES == 0
    nblk = t // blk
    return pl.pallas_call(
        functools.partial(_moba_seq_kernel, nblk=nblk),
        grid=(nb, hw // LANES, nblk),
        in_specs=[pl.BlockSpec((1, blk, LANES), lambda b, h, i: (b, i, h)),
                  pl.BlockSpec((1, t, LANES), lambda b, h, i: (b, 0, h)),
                  pl.BlockSpec((1, t, LANES), lambda b, h, i: (b, 0, h))],
        out_specs=pl.BlockSpec((1, blk, LANES), lambda b, h, i: (b, i, h)),
        out_shape=jax.ShapeDtypeStruct((nb, t, hw), BF16),
        scratch_shapes=[pltpu.VMEM((nblk, LANES), F32), pltpu.VMEM((2, nblk, blk, LANES), F32),
                        pltpu.VMEM((2, blk, LANES), BF16), pltpu.VMEM((2, blk, 1), F32),
                        pltpu.VMEM((2, blk, 1), F32), pltpu.VMEM((2, blk, LANES), F32)],
        compiler_params=_cparams("parallel", "parallel", "arbitrary"), name="moba_seq",
    )(q, kb, vb)


def _dsa_seq_kernel(qi_ref, kiw_ref, ki_ref, q_ref, k_ref, v_ref, o_ref, sc_s, key_s, m_s, l_s, acc_s,
                    *, tk, nsel, idx_bits):
    i = pl.program_id(1)
    tq = qi_ref.shape[1]
    nkv = DSA_KV_HEADS
    grp = q_ref.shape[2] // HEAD_DIM // nkv
    nck = ((i + 1) * tq + tk - 1) // tk
    qpos = i * tq + lax.broadcasted_iota(I32, (tq, 1), 0)
    lane_tk = lax.broadcasted_iota(I32, (1, tk), 1)

    qi = qi_ref[0]
    wi = kiw_ref[0][:, IDX_DIM:IDX_DIM + IDX_HEADS] * (IDX_HEADS ** -0.5)

    def score_chunk(c, carry):
        c0 = pl.multiple_of(c * tk, tk)
        kc = ki_ref[0, pl.ds(c0, tk), :]
        sc = jnp.zeros((tq, tk), F32)
        for h in range(IDX_HEADS):
            d = _dot_t(qi[:, h * IDX_DIM:(h + 1) * IDX_DIM], kc)
            sc = sc + wi[:, h:h + 1] * jnp.maximum(d, 0.0)
        sc = jnp.where(c0 + lane_tk <= qpos, sc, -jnp.inf)
        key_s[:, pl.ds(c0, tk)] = _sortable(sc)
        return carry

    lax.fori_loop(0, nck, score_chunk, 0)

    def count(pred):
        def body(c, acc):
            c0 = pl.multiple_of(c * tk, tk)
            hit = pred(key_s[:, pl.ds(c0, tk)], c0)
            for j in range(tk // LANES):
                acc = acc + hit[:, j * LANES:(j + 1) * LANES]
            return acc
        acc = lax.fori_loop(0, nck, body, jnp.zeros((tq, LANES), I32))
        return jnp.sum(acc, axis=1, keepdims=True)

    def value_bit(it, thr):
        cand = thr + jnp.left_shift(jnp.int32(1), 31 - it)
        cnt = count(lambda key, c0: jnp.where(key >= cand, 1, 0))
        return jnp.where(cnt >= nsel, cand, thr)

    thr = lax.fori_loop(0, 32, value_bit, jnp.full((tq, 1), INT_MIN, I32))
    need = nsel - count(lambda key, c0: jnp.where(key > thr, 1, 0))

    def index_bit(it, cut):
        cand = cut + jnp.left_shift(jnp.int32(1), idx_bits - 1 - it)
        cnt = count(lambda key, c0: jnp.where(key == thr, jnp.where(c0 + lane_tk < cand, 1, 0), 0))
        return jnp.where(cnt < need, cand, cut)

    cut = lax.fori_loop(0, idx_bits, index_bit, jnp.zeros((tq, 1), I32))

    def mask_chunk(c, carry):
        c0 = pl.multiple_of(c * tk, tk)
        key = key_s[:, pl.ds(c0, tk)]
        kpos = c0 + lane_tk
        take = jnp.where(key > thr, 1, jnp.where(key == thr, jnp.where(kpos <= cut, 1, 0), 0))
        take = jnp.where(kpos <= qpos, take, 0)
        sc_s[:, pl.ds(c0, tk)] = jnp.where(take > 0, 0.0, NEG)
        return carry

    lax.fori_loop(0, nck, mask_chunk, 0)

    q = q_ref[0]
    outs = [None] * (nkv * grp)
    for n in range(nkv):
        qn = jnp.concatenate([q[:, (n * grp + g) * HEAD_DIM:(n * grp + g + 1) * HEAD_DIM] for g in range(grp)],
                             axis=0)
        m_s[...] = jnp.full(m_s.shape, NEG, F32)
        l_s[...] = jnp.zeros(l_s.shape, F32)
        acc_s[...] = jnp.zeros(acc_s.shape, F32)

        def attend(c, carry, n=n, qn=qn):
            c0 = pl.multiple_of(c * tk, tk)
            kc = k_ref[0, pl.ds(c0, tk), :][:, n * HEAD_DIM:(n + 1) * HEAD_DIM]
            vc = v_ref[0, pl.ds(c0, tk), :][:, n * HEAD_DIM:(n + 1) * HEAD_DIM]
            bias = sc_s[:, pl.ds(c0, tk)]
            s = _dot_t(qn, kc) + jnp.concatenate([bias] * grp, axis=0)
            m_old = m_s[...]
            m_new = jnp.maximum(m_old, jnp.max(s, axis=1, keepdims=True))
            alpha = jnp.exp(m_old - m_new)
            p = jnp.exp(s - m_new)
            l_s[...] = alpha * l_s[...] + jnp.sum(p, axis=1, keepdims=True)
            acc_s[...] = alpha * acc_s[...] + _dot(p.astype(BF16), vc)
            m_s[...] = m_new
            return carry

        lax.fori_loop(0, nck, attend, 0)
        on = acc_s[...] / l_s[...]
        for g in range(grp):
            outs[n * grp + g] = on[g * tq:(g + 1) * tq]
    o_ref[0] = jnp.concatenate(outs, axis=1).astype(o_ref.dtype)


def _dsa_seq(qib, kiw, kib, qb, kb, vb, tq, tk):
    nb, t, qw = qb.shape
    assert t % tq == 0 and t % tk == 0 and tk % LANES == 0
    nsel = min(DSA_TOPK, t // 4)
    kvw = kb.shape[2]
    grp = qw // HEAD_DIM // DSA_KV_HEADS
    return pl.pallas_call(
        functools.partial(_dsa_seq_kernel, tk=tk, nsel=nsel, idx_bits=max(1, (t - 1).bit_length())),
        grid=(nb, t // tq),
        in_specs=[pl.BlockSpec((1, tq, qib.shape[2]), lambda b, i: (b, i, 0)),
                  pl.BlockSpec((1, tq, LANES), lambda b, i: (b, i, 0)),
                  pl.BlockSpec((1, t, IDX_DIM), lambda b, i: (b, 0, 0)),
                  pl.BlockSpec((1, tq, qw), lambda b, i: (b, i, 0)),
                  pl.BlockSpec((1, t, kvw), lambda b, i: (b, 0, 0)),
                  pl.BlockSpec((1, t, kvw), lambda b, i: (b, 0, 0))],
        out_specs=pl.BlockSpec((1, tq, qw), lambda b, i: (b, i, 0)),
        out_shape=jax.ShapeDtypeStruct((nb, t, qw), BF16),
        scratch_shapes=[pltpu.VMEM((tq, t), F32), pltpu.VMEM((tq, t), I32),
                        pltpu.VMEM((grp * tq, 1), F32), pltpu.VMEM((grp * tq, 1), F32),
                        pltpu.VMEM((grp * tq, HEAD_DIM), F32)],
        compiler_params=_cparams("parallel", "arbitrary"), name="dsa_seq",
    )(qib, kiw, kib, qb, kb, vb)


def _ffn_core(x1, gn_ref, wg_ref, wu_ref, cw_ref, cb_ref, wd_ref, conv_fn, fc):
    xn = _rms(x1, gn_ref[...]).astype(BF16)
    dff = wg_ref.shape[1]
    acc = jnp.zeros(x1.shape, F32)
    for c in range(dff // fc):
        cs = slice(c * fc, (c + 1) * fc)
        g = _dot(xn, wg_ref[:, cs])
        u = _dot(xn, wu_ref[:, cs])
        gm2, gm1 = conv_fn(g, cs)
        gc = gm2 * cw_ref[0:1, cs] + gm1 * cw_ref[1:2, cs] + g * cw_ref[2:3, cs] + cb_ref[:, cs]
        acc = acc + _dot((_gelu(gc) * u).astype(BF16), wd_ref[cs, :])
    return x1 + acc


def _ffn_seq_kernel(*refs, n_y, tiles_per_seq, final, fc):
    x_ref = refs[0]
    y_refs = refs[1:1 + n_y]
    wo_refs = refs[1 + n_y:1 + 2 * n_y]
    gn_ref, wg_ref, wu_ref, cw_ref, cb_ref, wd_ref, gf_ref = refs[1 + 2 * n_y:8 + 2 * n_y]
    out_ref, st_ref, tail_s = refs[8 + 2 * n_y:]
    tm = x_ref.shape[0]

    @pl.when(pl.program_id(0) % tiles_per_seq == 0)
    def _():
        tail_s[...] = jnp.zeros_like(tail_s)

    x1 = x_ref[...]
    for y_ref, wo_ref in zip(y_refs, wo_refs):
        x1 = x1 + _dot(y_ref[...], wo_ref[...])

    def conv_fn(g, cs):
        prev = tail_s[:, cs]
        tail_s[:, cs] = g[tm - SUBLANES:tm]
        return _shift_rows(g, prev, 2), _shift_rows(g, prev, 1)

    x2 = _ffn_core(x1, gn_ref, wg_ref, wu_ref, cw_ref, cb_ref, wd_ref, conv_fn, fc)
    st_ref[0] = tail_s[...]
    out_ref[...] = _rms(x2, gf_ref[...]) if final else x2


def _ffn_step_kernel(*refs, n_y, final, fc):
    x_ref = refs[0]
    y_refs = refs[1:1 + n_y]
    wo_refs = refs[1 + n_y:1 + 2 * n_y]
    gn_ref, wg_ref, wu_ref, cw_ref, cb_ref, wd_ref, gf_ref, p_ref = refs[1 + 2 * n_y:9 + 2 * n_y]
    out_ref, g_ref = refs[9 + 2 * n_y:]
    dff = wg_ref.shape[1]

    x1 = x_ref[...]
    for y_ref, wo_ref in zip(y_refs, wo_refs):
        x1 = x1 + _dot(y_ref[...], wo_ref[...])

    def conv_fn(g, cs):
        g_ref[:, cs] = g
        return p_ref[:, cs], p_ref[:, dff + cs.start:dff + cs.stop]

    x2 = _ffn_core(x1, gn_ref, wg_ref, wu_ref, cw_ref, cb_ref, wd_ref, conv_fn, fc)
    out_ref[...] = _rms(x2, gf_ref[...]) if final else x2


def _ffn_weights_specs(d, dff):
    return [_const_spec((1, d)), _const_spec((d, dff)), _const_spec((d, dff)), _const_spec((3, dff)),
            _const_spec((1, dff)), _const_spec((dff, d)), _const_spec((1, d))]


def _ffn_seq(x2d, ys, wos, gn, wg, wu, cw, cb, wd, gfinal, final, seq_len, tm, fc=1024):
    n, d = x2d.shape
    dff = wg.shape[1]
    assert seq_len % tm == 0 and dff % fc == 0
    tps = seq_len // tm
    row_spec = lambda c: pl.BlockSpec((tm, c), lambda i: (i, 0))
    return pl.pallas_call(
        functools.partial(_ffn_seq_kernel, n_y=len(ys), tiles_per_seq=tps, final=final, fc=fc),
        grid=(n // tm,),
        in_specs=[row_spec(d)] + [row_spec(y.shape[1]) for y in ys] + [_const_spec(w.shape) for w in wos]
                 + _ffn_weights_specs(d, dff),
        out_specs=[row_spec(d), pl.BlockSpec((1, SUBLANES, dff), lambda i: (i // tps, 0, 0))],
        out_shape=[jax.ShapeDtypeStruct((n, d), F32), jax.ShapeDtypeStruct((n // seq_len, SUBLANES, dff), F32)],
        scratch_shapes=[pltpu.VMEM((SUBLANES, dff), F32)],
        compiler_params=_cparams("arbitrary"), name="ffn_seq",
    )(x2d, *ys, *wos, gn.reshape(1, d), wg, wu, cw, cb.reshape(1, dff), wd, gfinal.reshape(1, d))


def _ffn_step(x2d, ys, wos, gn, wg, wu, cw, cb, wd, gfinal, final, prev, fc=1024):
    n, d = x2d.shape
    dff = wg.shape[1]
    full = lambda a: pl.BlockSpec(a.shape, lambda i: (0,) * a.ndim)
    prev2 = prev.reshape(n, 2 * dff)
    return pl.pallas_call(
        functools.partial(_ffn_step_kernel, n_y=len(ys), final=final, fc=fc),
        grid=(1,),
        in_specs=[full(x2d)] + [full(y) for y in ys] + [_const_spec(w.shape) for w in wos]
                 + _ffn_weights_specs(d, dff) + [full(prev2)],
        out_specs=[pl.BlockSpec((n, d), lambda i: (0, 0)), pl.BlockSpec((n, dff), lambda i: (0, 0))],
        out_shape=[jax.ShapeDtypeStruct((n, d), F32), jax.ShapeDtypeStruct((n, dff), F32)],
        compiler_params=_cparams("arbitrary"), name="ffn_step",
    )(x2d, *ys, *wos, gn.reshape(1, d), wg, wu, cw, cb.reshape(1, dff), wd, gfinal.reshape(1, d), prev2)


def _moba_pick_kernel(pt_ref, q_ref, *refs, pages_per_step):
    page_refs = refs[:pages_per_step]
    sel_ref, tv_s, ti_s = refs[pages_per_step:]
    s = pl.program_id(1)
    ppb = MOBA_BLOCK // PAGE_SIZE
    bps = pages_per_step // ppb
    heads, hd = q_ref.shape[1], q_ref.shape[2]

    @pl.when(s == 0)
    def _():
        tv_s[...] = jnp.full(tv_s.shape, -jnp.inf, F32)
        ti_s[...] = jnp.zeros(ti_s.shape, I32)

    own = (lax.broadcasted_iota(I32, (heads, heads * hd), 0) == lax.broadcasted_iota(I32, (heads, heads * hd), 1) // hd)
    qbd = jnp.where(own, jnp.concatenate([q_ref[0]] * heads, axis=1), 0.0)
    t1, t2, t3 = tv_s[0], tv_s[1], tv_s[2]
    i1, i2, i3 = ti_s[0], ti_s[1], ti_s[2]
    for j in range(bps):
        tot = jnp.zeros((heads, 1), F32)
        for r in range(ppb):
            kt = page_refs[j * ppb + r][0].reshape(heads * hd, PAGE_SIZE)
            sc = jnp.dot(qbd, kt, precision=_HI, preferred_element_type=F32)
            tot = tot + jnp.sum(sc, axis=1, keepdims=True)
        g = tot * (1.0 / MOBA_BLOCK)
        n = s * bps + j
        c1, c2, c3 = g > t1, g > t2, g > t3
        t3, i3 = jnp.where(c2, t2, jnp.where(c3, g, t3)), jnp.where(c2, i2, jnp.where(c3, n, i3))
        t2, i2 = jnp.where(c1, t1, jnp.where(c2, g, t2)), jnp.where(c1, i1, jnp.where(c2, n, i2))
        t1, i1 = jnp.where(c1, g, t1), jnp.where(c1, n, i1)
    tv_s[0], tv_s[1], tv_s[2] = t1, t2, t3
    ti_s[0], ti_s[1], ti_s[2] = i1, i2, i3

    @pl.when(s == pl.num_programs(1) - 1)
    def _():
        lane = lax.broadcasted_iota(I32, (heads, LANES), 1)
        sel_ref[0] = jnp.where(lane == 0, i1, jnp.where(lane == 1, i2, i3))


def _moba_pick(page_table, q, cache_kt, pages_per_step=16):
    nb, npg = page_table.shape
    _, heads, hd, _ = cache_kt.shape
    ppb = MOBA_BLOCK // PAGE_SIZE
    assert MOBA_TOPK == 3 and npg % pages_per_step == 0 and pages_per_step % ppb == 0 and npg // ppb >= MOBA_TOPK
    page_spec = lambda j: pl.BlockSpec((1, heads, hd, PAGE_SIZE),
                                       lambda b, s, pt: (pt[b, s * pages_per_step + j], 0, 0, 0))
    grid_spec = pltpu.PrefetchScalarGridSpec(
        num_scalar_prefetch=1, grid=(nb, npg // pages_per_step),
        in_specs=[pl.BlockSpec((1, heads, hd), lambda b, s, pt: (b, 0, 0))]
                 + [page_spec(j) for j in range(pages_per_step)],
        out_specs=pl.BlockSpec((1, heads, LANES), lambda b, s, pt: (b, 0, 0)),
        scratch_shapes=[pltpu.VMEM((MOBA_TOPK, heads, 1), F32), pltpu.VMEM((MOBA_TOPK, heads, 1), I32)])
    return pl.pallas_call(
        functools.partial(_moba_pick_kernel, pages_per_step=pages_per_step),
        grid_spec=grid_spec, out_shape=jax.ShapeDtypeStruct((nb, heads, LANES), I32),
        compiler_params=_cparams("parallel", "arbitrary"), name="moba_pick",
    )(page_table, q, *([cache_kt] * pages_per_step))


def _moba_step_kernel(sel_ref, pt_ref, q_ref, kn_ref, vn_ref, *refs, npage):
    k_refs = refs[:npage]
    v_refs = refs[npage:2 * npage]
    o_ref = refs[2 * npage]
    hd = q_ref.shape[3]
    q8 = jnp.broadcast_to(q_ref[0, 0] * (HEAD_DIM ** -0.5), (SUBLANES, hd))
    m = jnp.sum(q8 * kn_ref[0, 0], axis=1, keepdims=True)
    l = jnp.ones_like(m)
    acc = jnp.broadcast_to(vn_ref[0, 0], (SUBLANES, hd))
    for j in range(npage):
        s = jnp.dot(q8, k_refs[j][0, 0], precision=_HI, preferred_element_type=F32)
        m_new = jnp.maximum(m, jnp.max(s, axis=1, keepdims=True))
        alpha = jnp.exp(m - m_new)
        p = jnp.exp(s - m_new)
        l = alpha * l + jnp.sum(p, axis=1, keepdims=True)
        acc = alpha * acc + _dot_t(p, v_refs[j][0, 0], precision=_HI)
        m = m_new
    o_ref[0, 0] = (acc / l)[0:1]


def _moba_step(sel, page_table, q, k_new, v_new, cache_kt, cache_vt):
    nb, heads, hd = q.shape
    ppb = MOBA_BLOCK // PAGE_SIZE
    npage = MOBA_TOPK * ppb

    def page_spec(j):
        r, pg = divmod(j, ppb)
        return pl.BlockSpec((1, 1, hd, PAGE_SIZE),
                            lambda b, h, sel, pt: (pt[b, sel[b, h * MOBA_TOPK + r] * ppb + pg], h, 0, 0))

    vec = pl.BlockSpec((1, 1, 1, hd), lambda b, h, sel, pt: (b, h, 0, 0))
    grid_spec = pltpu.PrefetchScalarGridSpec(
        num_scalar_prefetch=2, grid=(nb, heads),
        in_specs=[vec, vec, vec] + [page_spec(j) for j in range(npage)] * 2,
        out_specs=vec)
    r4 = lambda a: a.reshape(nb, heads, 1, hd)
    return pl.pallas_call(
        functools.partial(_moba_step_kernel, npage=npage),
        grid_spec=grid_spec, out_shape=jax.ShapeDtypeStruct((nb, heads, 1, hd), F32),
        compiler_params=_cparams("parallel", "arbitrary"), name="moba_step",
    )(sel, page_table, r4(q), r4(k_new), r4(v_new), *([cache_kt] * npage), *([cache_vt] * npage)).reshape(nb, heads * hd)


def _dsa_score_kernel(pt_ref, qi_ref, w_ref, *refs, pages_per_step):
    page_refs = refs[:pages_per_step]
    sc_ref = refs[pages_per_step]
    qi = qi_ref[0] * (IDX_DIM ** -0.5)
    w = w_ref[0] * (IDX_HEADS ** -0.5)
    for j in range(pages_per_step):
        d = jnp.dot(qi, page_refs[j][0], precision=_HI, preferred_element_type=F32)
        sc_ref[0, j:j + 1, :] = jnp.sum(w * jnp.maximum(d, 0.0), axis=0, keepdims=True)


def _dsa_score(page_table, qi, wi, cache_ki, pages_per_step=16):
    nb, npg = page_table.shape
    assert npg % pages_per_step == 0
    page_spec = lambda j: pl.BlockSpec((1, IDX_DIM, PAGE_SIZE), lambda b, s, pt: (pt[b, s * pages_per_step + j], 0, 0))
    grid_spec = pltpu.PrefetchScalarGridSpec(
        num_scalar_prefetch=1, grid=(nb, npg // pages_per_step),
        in_specs=[pl.BlockSpec((1, IDX_HEADS, IDX_DIM), lambda b, s, pt: (b, 0, 0)),
                  pl.BlockSpec((1, IDX_HEADS, 1), lambda b, s, pt: (b, 0, 0))]
                 + [page_spec(j) for j in range(pages_per_step)],
        out_specs=pl.BlockSpec((1, pages_per_step, PAGE_SIZE), lambda b, s, pt: (b, s, 0)))
    return pl.pallas_call(
        functools.partial(_dsa_score_kernel, pages_per_step=pages_per_step),
        grid_spec=grid_spec, out_shape=jax.ShapeDtypeStruct((nb, npg, PAGE_SIZE), F32),
        compiler_params=_cparams("parallel", "arbitrary"), name="dsa_score",
    )(page_table, qi, wi, *([cache_ki] * pages_per_step))


def _dsa_step_kernel(pt_ref, sc_ref, qi_ref, w_ref, kin_ref, q_ref, kn_ref, vn_ref, *refs,
                     pages_per_step, nsel, idx_bits):
    k_refs = refs[:pages_per_step]
    v_refs = refs[pages_per_step:2 * pages_per_step]
    o_ref, bias_s, m_s, l_s, acc_s = refs[2 * pages_per_step:]
    s = pl.program_id(1)
    nh = q_ref.shape[1]
    kvw = kn_ref.shape[2]
    grp = nh // DSA_KV_HEADS
    npg = sc_ref.shape[1]
    row_kv = lax.broadcasted_iota(I32, (nh, kvw), 0) // grp
    lane_kv = lax.broadcasted_iota(I32, (nh, kvw), 1) // HEAD_DIM
    own = row_kv == lane_kv
    q = q_ref[0] * (HEAD_DIM ** -0.5)
    qbd = jnp.where(own, jnp.concatenate([q] * DSA_KV_HEADS, axis=1), 0.0)

    @pl.when(s == 0)
    def _():
        qi = qi_ref[0] * (IDX_DIM ** -0.5)
        d = jnp.sum(qi * kin_ref[0], axis=1, keepdims=True)
        sc_new = jnp.sum(w_ref[0] * (IDX_HEADS ** -0.5) * jnp.maximum(d, 0.0), axis=0, keepdims=True)
        key = _sortable(sc_ref[0])
        key_new = _sortable(sc_new)
        pos = (lax.broadcasted_iota(I32, key.shape, 0) * PAGE_SIZE + lax.broadcasted_iota(I32, key.shape, 1))
        pos_new = npg * PAGE_SIZE

        def total(hit, hit_new):
            t = jnp.sum(jnp.sum(hit, axis=1, keepdims=True), axis=0, keepdims=True)
            return t + hit_new

        def value_bit(it, thr):
            cand = thr + jnp.left_shift(jnp.int32(1), 31 - it)
            cnt = total(jnp.where(key >= cand, 1, 0), jnp.where(key_new >= cand, 1, 0))
            return jnp.where(cnt >= nsel, cand, thr)

        thr = lax.fori_loop(0, 32, value_bit, jnp.full((1, 1), INT_MIN, I32))
        need = nsel - total(jnp.where(key > thr, 1, 0), jnp.where(key_new > thr, 1, 0))

        def index_bit(it, cut):
            cand = cut + jnp.left_shift(jnp.int32(1), idx_bits - 1 - it)
            cnt = total(jnp.where(key == thr, jnp.where(pos < cand, 1, 0), 0),
                        jnp.where(key_new == thr, jnp.where(pos_new < cand, 1, 0), 0))
            return jnp.where(cnt < need, cand, cut)

        cut = lax.fori_loop(0, idx_bits, index_bit, jnp.zeros((1, 1), I32))
        take = jnp.where(key > thr, 1, jnp.where(key == thr, jnp.where(pos <= cut, 1, 0), 0))
        bias_s[...] = jnp.where(take > 0, 0.0, NEG)
        take_new = jnp.where(key_new > thr, 1, jnp.where(key_new == thr, jnp.where(pos_new <= cut, 1, 0), 0))
        bias_new = jnp.where(take_new > 0, 0.0, NEG)
        m_s[...] = jnp.sum(qbd * kn_ref[0], axis=1, keepdims=True) + bias_new
        l_s[...] = jnp.ones_like(l_s)
        acc_s[...] = jnp.broadcast_to(vn_ref[0], acc_s.shape)

    m, l, acc = m_s[...], l_s[...], acc_s[...]
    for j in range(pages_per_step):
        kt = k_refs[j][0].reshape(kvw, PAGE_SIZE)
        vt = v_refs[j][0].reshape(kvw, PAGE_SIZE)
        sc = (jnp.dot(qbd, kt, precision=_HI, preferred_element_type=F32)
              + bias_s[pl.ds(s * pages_per_step + j, 1), :])
        m_new = jnp.maximum(m, jnp.max(sc, axis=1, keepdims=True))
        alpha = jnp.exp(m - m_new)
        p = jnp.exp(sc - m_new)
        l = alpha * l + jnp.sum(p, axis=1, keepdims=True)
        acc = alpha * acc + _dot_t(p, vt, precision=_HI)
        m = m_new
    m_s[...], l_s[...], acc_s[...] = m, l, acc

    @pl.when(s == pl.num_programs(1) - 1)
    def _():
        on = jnp.where(own, acc / l, 0.0)
        o = on[:, 0:HEAD_DIM]
        for n in range(1, DSA_KV_HEADS):
            o = o + on[:, n * HEAD_DIM:(n + 1) * HEAD_DIM]
        o_ref[0] = o


def _dsa_step(page_table, scores, qi, wi, ki_new, q, k_new, v_new, cache_k, cache_v, pages_per_step=16):
    nb, npg = page_table.shape
    nh = q.shape[1]
    kvw = k_new.shape[1]
    total_len = npg * PAGE_SIZE + 1
    nsel = min(DSA_TOPK, total_len // 4)
    bmap = lambda b, s, pt: (b, 0, 0)
    page_spec = lambda j: pl.BlockSpec((1, DSA_KV_HEADS, HEAD_DIM, PAGE_SIZE),
                                       lambda b, s, pt: (pt[b, s * pages_per_step + j], 0, 0, 0))
    grid_spec = pltpu.PrefetchScalarGridSpec(
        num_scalar_prefetch=1, grid=(nb, npg // pages_per_step),
        in_specs=[pl.BlockSpec((1, npg, PAGE_SIZE), bmap), pl.BlockSpec((1, IDX_HEADS, IDX_DIM), bmap),
                  pl.BlockSpec((1, IDX_HEADS, 1), bmap), pl.BlockSpec((1, 1, IDX_DIM), bmap),
                  pl.BlockSpec((1, nh, HEAD_DIM), bmap), pl.BlockSpec((1, 1, kvw), bmap), pl.BlockSpec((1, 1, kvw), bmap)]
                 + [page_spec(j) for j in range(pages_per_step)] * 2,
        out_specs=pl.BlockSpec((1, nh, HEAD_DIM), bmap),
        scratch_shapes=[pltpu.VMEM((npg, PAGE_SIZE), F32), pltpu.VMEM((nh, 1), F32), pltpu.VMEM((nh, 1), F32),
                        pltpu.VMEM((nh, kvw), F32)])
    return pl.pallas_call(
        functools.partial(_dsa_step_kernel, pages_per_step=pages_per_step, nsel=nsel,
                          idx_bits=max(1, (total_len - 1).bit_length())),
        grid_spec=grid_spec, out_shape=jax.ShapeDtypeStruct((nb, nh, HEAD_DIM), F32),
        compiler_params=_cparams("parallel", "arbitrary"), name="dsa_step",
    )(page_table, scores, qi, wi, ki_new.reshape(nb, 1, IDX_DIM), q, k_new.reshape(nb, 1, kvw),
      v_new.reshape(nb, 1, kvw), *([cache_k] * pages_per_step), *([cache_v] * pages_per_step))


def _in0_segs(w):
    return ((0, 2 * w, None, 1.0, ((F32, LANES),)),
            (2 * w, w, "heads", 1.0, ((F32, LANES),)),
            (3 * w, w, "heads", 1.0, ((F32, LANES), (BF16, LANES))),
            (4 * w, w, None, 1.0, ((F32, LANES), (BF16, LANES))))


def _in1_segs(qw, kvw, iw, step):
    q_out = ((F32, LANES),) if step else ((BF16, LANES),)
    return ((0, qw, "heads", 1.0 if step else HEAD_DIM ** -0.5, q_out),
            (qw, kvw, "heads", 1.0, ((F32, LANES), (BF16, LANES))),
            (qw + kvw, kvw, None, 1.0, ((F32, LANES), (BF16, LANES))),
            (qw + 2 * kvw, iw, "heads", 1.0 if step else IDX_DIM ** -0.5, q_out),
            (qw + 2 * kvw + iw, LANES, "half", 1.0, ((F32, LANES), (F32, IDX_DIM), (BF16, IDX_DIM))))


def _prep_weights(p):
    d = p["w_in0"].shape[0]
    w = p["rg_lambda"].shape[0]
    nblocks = p["rg_gate_a_w"].shape[0]
    eye = jnp.eye(nblocks, dtype=F32)
    bd = lambda g: jnp.einsum("ncd,nm->ncmd", g, eye).reshape(w, w).astype(BF16)
    w_in1 = p["w_in1"]
    pad = (-w_in1.shape[1]) % LANES
    return dict(
        w_in0=p["w_in0"].astype(BF16), wa=bd(p["rg_gate_a_w"]), wx=bd(p["rg_gate_x_w"]),
        w_out0a=p["w_out0"][:w].astype(BF16), w_out0b=p["w_out0"][w:].astype(BF16),
        w_in1=jnp.pad(w_in1, ((0, 0), (0, pad))).astype(BF16), w_out1=p["w_out1"].astype(BF16),
        wg=p["ffn_w_gate"].astype(BF16), wu=p["ffn_w_up"].astype(BF16), wd=p["ffn_w_down"].astype(BF16))


def _prompt_group(x, p, wb):
    nb, t, d = x.shape
    n = nb * t
    w = p["rg_lambda"].shape[0]
    tm = 512
    tabs = _rope_tables(jnp.arange(t, dtype=I32), t)
    x2 = x.reshape(n, d)

    ug, q0, k0, k0b, v0, v0b = _norm_proj(x2, p["norm_mix"][0], wb["w_in0"], tabs, _in0_segs(w), tm, t // tm)
    ug3 = ug.reshape(nb, t, 2 * w)
    y_rg, h_last = _rglru_seq(ug3, p["rg_conv_w"], p["rg_conv_b"], wb["wa"], wb["wx"], p["rg_gate_a_b"],
                              p["rg_gate_x_b"], p["rg_lambda"], tc=256)
    o0 = _moba_seq(q0.reshape(nb, t, w), k0b.reshape(nb, t, w), v0b.reshape(nb, t, w))
    x3, st0 = _ffn_seq(x2, [y_rg.reshape(n, w), o0.reshape(n, w)], [wb["w_out0a"], wb["w_out0b"]],
                       p["norm_ffn"][0], wb["wg"][0], wb["wu"][0], p["ffn_conv_w"][0], p["ffn_conv_b"][0],
                       wb["wd"][0], p["norm_final"], False, t, tm)

    qw = p["w_out1"].shape[0]
    kvw = DSA_KV_HEADS * HEAD_DIM
    iw = IDX_HEADS * IDX_DIM
    q1b, k1, k1b, v1, v1b, qib, kiw, ki, kib = _norm_proj(
        x3, p["norm_mix"][1], wb["w_in1"], tabs, _in1_segs(qw, kvw, iw, False), tm, t // tm)
    r3 = lambda a: a.reshape(nb, t, a.shape[1])
    o1 = _dsa_seq(r3(qib), r3(kiw), r3(kib), r3(q1b), r3(k1b), r3(v1b), tq=128, tk=512)
    y, st1 = _ffn_seq(x3, [o1.reshape(n, qw)], [wb["w_out1"]], p["norm_ffn"][1], wb["wg"][1], wb["wu"][1],
                      p["ffn_conv_w"][1], p["ffn_conv_b"][1], wb["wd"][1], p["norm_final"], True, t, tm)

    heads = w // HEAD_DIM
    ffn_state = jnp.stack([st0[:, SUBLANES - 2:], st1[:, SUBLANES - 2:]])
    return (y.reshape(nb, t, d), k0.reshape(nb, t, heads, HEAD_DIM), v0.reshape(nb, t, heads, HEAD_DIM), h_last,
            ug3[:, t - 3:, :w], k1.reshape(nb, t, DSA_KV_HEADS, HEAD_DIM), v1.reshape(nb, t, DSA_KV_HEADS, HEAD_DIM),
            ki.reshape(nb, t, IDX_DIM), ffn_state)


def _sample_group(x, p, wb, cache_k0, cache_v0, state_h0, state_conv0, cache_k1, cache_v1, cache_kidx1,
                  state_ffn, page_table):
    nb, t, d = x.shape
    assert t == 1
    w = p["rg_lambda"].shape[0]
    npool = cache_k0.shape[0]
    npg = page_table.shape[1]
    assert (npg * PAGE_SIZE) % MOBA_BLOCK == 0
    tabs = _rope_tables(jnp.full((1,), npg * PAGE_SIZE, I32), nb)
    x2 = x.reshape(nb, d)

    ug, q0, k0, _, v0, _ = _norm_proj(x2, p["norm_mix"][0], wb["w_in0"], tabs, _in0_segs(w), nb, 1)
    y_rg, h_new = _rglru_step(ug, state_conv0, state_h0, p["rg_conv_w"], p["rg_conv_b"], wb["wa"], wb["wx"],
                              p["rg_gate_a_b"], p["rg_gate_x_b"], p["rg_lambda"])
    heads = w // HEAD_DIM
    paged_t = lambda c: c.transpose(0, 2, 3, 1)
    ck0, cv0 = paged_t(cache_k0), paged_t(cache_v0)
    hsplit = lambda a: a.reshape(nb, heads, HEAD_DIM)
    picked = _moba_pick(page_table, hsplit(q0), ck0)
    sel = picked[:, :, :MOBA_TOPK].reshape(nb, heads * MOBA_TOPK)
    o0 = _moba_step(sel, page_table, hsplit(q0), hsplit(k0), hsplit(v0), ck0, cv0)
    x3, g0 = _ffn_step(x2, [y_rg, o0.astype(BF16)], [wb["w_out0a"], wb["w_out0b"]], p["norm_ffn"][0], wb["wg"][0],
                       wb["wu"][0], p["ffn_conv_w"][0], p["ffn_conv_b"][0], wb["wd"][0], p["norm_final"], False,
                       state_ffn[0])

    qw = p["w_out1"].shape[0]
    kvw = DSA_KV_HEADS * HEAD_DIM
    iw = IDX_HEADS * IDX_DIM
    q1, k1, _, v1, _, qi, kiw, ki, _ = _norm_proj(
        x3, p["norm_mix"][1], wb["w_in1"], tabs, _in1_segs(qw, kvw, iw, True), nb, 1)
    qi3 = qi.reshape(nb, IDX_HEADS, IDX_DIM)
    wi3 = kiw[:, IDX_DIM:IDX_DIM + IDX_HEADS].reshape(nb, IDX_HEADS, 1)
    scores = _dsa_score(page_table, qi3, wi3, cache_kidx1.transpose(0, 2, 1))
    o1 = _dsa_step(page_table, scores, qi3, wi3, ki, q1.reshape(nb, qw // HEAD_DIM, HEAD_DIM), k1, v1,
                   paged_t(cache_k1), paged_t(cache_v1))
    y, g1 = _ffn_step(x3, [o1.reshape(nb, qw).astype(BF16)], [wb["w_out1"]], p["norm_ffn"][1], wb["wg"][1],
                      wb["wu"][1], p["ffn_conv_w"][1], p["ffn_conv_b"][1], wb["wd"][1], p["norm_final"], True,
                      state_ffn[1])

    conv_new = jnp.concatenate([state_conv0[:, 1:], ug[:, None, :w]], axis=1)
    ffn_state = jnp.stack([jnp.stack([state_ffn[0][:, 1], g0], axis=1), jnp.stack([state_ffn[1][:, 1], g1], axis=1)])
    return (y.reshape(nb, 1, d), k0.reshape(nb, 1, heads, HEAD_DIM), v0.reshape(nb, 1, heads, HEAD_DIM), h_new,
            conv_new, k1.reshape(nb, 1, DSA_KV_HEADS, HEAD_DIM), v1.reshape(nb, 1, DSA_KV_HEADS, HEAD_DIM),
            ki.reshape(nb, 1, IDX_DIM), ffn_state)


def kernel(x_prompt, x_sample, cache_k0, cache_v0, state_h0, state_conv0, cache_k1, cache_v1, cache_kidx1,
           state_ffn, page_table, norm_mix, norm_ffn, norm_final, w_in0, rg_conv_w, rg_conv_b, rg_gate_a_w,
           rg_gate_a_b, rg_gate_x_w, rg_gate_x_b, rg_lambda, w_out0, w_in1, w_out1, ffn_w_gate, ffn_w_up,
           ffn_conv_w, ffn_conv_b, ffn_w_down):
    p = dict(norm_mix=norm_mix, norm_ffn=norm_ffn, norm_final=norm_final, w_in0=w_in0, rg_conv_w=rg_conv_w,
             rg_conv_b=rg_conv_b, rg_gate_a_w=rg_gate_a_w, rg_gate_a_b=rg_gate_a_b, rg_gate_x_w=rg_gate_x_w,
             rg_gate_x_b=rg_gate_x_b, rg_lambda=rg_lambda, w_out0=w_out0, w_in1=w_in1, w_out1=w_out1,
             ffn_w_gate=ffn_w_gate, ffn_w_up=ffn_w_up, ffn_conv_w=ffn_conv_w, ffn_conv_b=ffn_conv_b,
             ffn_w_down=ffn_w_down)
    wb = _prep_weights(p)
    (y_p, k0_p, v0_p, h0_p, conv0_p, k1_p, v1_p, kidx1_p, ffn_p) = _prompt_group(x_prompt, p, wb)
    (y_s, k0_s, v0_s, h0_s, conv0_s, k1_s, v1_s, kidx1_s, ffn_s) = _sample_group(
        x_sample, p, wb, cache_k0, cache_v0, state_h0, state_conv0, cache_k1, cache_v1, cache_kidx1, state_ffn,
        page_table)
    return (y_p, y_s, k0_p, v0_p, h0_p, conv0_p, k1_p, v1_p, kidx1_p, ffn_p,
            k0_s, v0_s, h0_s, conv0_s, k1_s, v1_s, kidx1_s, ffn_s)
```

```python
import functools

import jax
import jax.numpy as jnp
from jax import lax
from jax.experimental import pallas as pl
from jax.experimental.pallas import tpu as pltpu

F32 = jnp.float32
BF16 = jnp.bfloat16
I32 = jnp.int32

HEAD_DIM = 64
PAGE_SIZE = 128
RG_C = 8.0
MOBA_BLOCK = 256
MOBA_TOPK = 3
DSA_KV_HEADS = 4
IDX_HEADS = 8
IDX_DIM = 64
DSA_TOPK = 256
ROPE_THETA = 10000.0
EPS = 1e-6

LANES = 128
SUBLANES = 8
VMEM_LIMIT = 56 * 1024 * 1024
NEG = -1e30
INT_MIN = -(2 ** 31)

_HI = lax.Precision.HIGHEST


def _cparams(*sem):
    return pltpu.CompilerParams(dimension_semantics=sem, vmem_limit_bytes=VMEM_LIMIT)


def _const_spec(shape):
    nd = len(shape)
    return pl.BlockSpec(shape, lambda *_: (0,) * nd, pipeline_mode=pl.Buffered(1))


def _gelu(x):
    return x * (0.5 * (1.0 + jnp.tanh(0.7978845608028654 * (x + 0.044715 * (x * x * x)))))


def _sigmoid(x):
    return 1.0 / (1.0 + jnp.exp(-x))


def _rms(x, g):
    return x * lax.rsqrt(jnp.mean(x * x, axis=-1, keepdims=True) + EPS) * g


def _dot(a, b):
    return jnp.dot(a, b, preferred_element_type=F32)


def _dot_t(a, b, precision=None):
    return lax.dot_general(a, b, (((1,), (1,)), ((), ())), precision=precision,
                           preferred_element_type=F32)


def _rope_group(x, cos, sin_signed):
    lane = lax.broadcasted_iota(I32, x.shape, 1)
    first_half = (lane % HEAD_DIM) < (HEAD_DIM // 2)
    partner = jnp.where(first_half, pltpu.roll(x, LANES - HEAD_DIM // 2, 1), pltpu.roll(x, HEAD_DIM // 2, 1))
    return x * cos + partner * sin_signed


def _shift_rows(x, prev, j):
    r = pltpu.roll(x, j, 0)
    row = lax.broadcasted_iota(I32, (SUBLANES, x.shape[1]), 0)
    head = jnp.where(row < j, pltpu.roll(prev, j, 0), r[0:SUBLANES])
    if x.shape[0] == SUBLANES:
        return head
    return jnp.concatenate([head, r[SUBLANES:]], axis=0)


def _sortable(x):
    b = pltpu.bitcast(x, I32)
    return b ^ ((b >> 31) & 0x7FFFFFFF)


def _norm_proj_kernel(x_ref, g_ref, w_ref, cos_ref, sin_ref, cosk_ref, sink_ref, *out_refs, segs):
    xb = _rms(x_ref[...], g_ref[...]).astype(BF16)
    oi = 0
    for c0, width, rope, scale, outs in segs:
        for g0 in range(0, width, LANES):
            y = _dot(xb, w_ref[:, c0 + g0:c0 + g0 + LANES])
            if rope == "heads":
                y = _rope_group(y, cos_ref[...], sin_ref[...])
            elif rope == "half":
                y = _rope_group(y, cosk_ref[...], sink_ref[...])
            if scale != 1.0:
                y = y * scale
            yt = y.T if any(kind.endswith("cols") for _, kind in outs) else None
            for k, (dt, kind) in enumerate(outs):
                o_ref = out_refs[oi + k]
                if kind == "rows":
                    o_ref[:, g0:g0 + LANES] = y.astype(dt)
                elif kind == "cols":
                    o_ref[0, g0:g0 + LANES, :] = yt.astype(dt)
                elif kind == "half_rows":
                    o_ref[...] = y[:, :LANES // 2].astype(dt)
                else:
                    o_ref[0] = yt[:LANES // 2].astype(dt)
        oi += len(outs)


def _norm_proj(x2d, gamma, w_bf16, tables, segs, tm, n_tab):
    n, d = x2d.shape
    seq = n_tab * tm
    out_shape, out_specs = [], []
    for c0, width, rope, scale, outs in segs:
        for dt, kind in outs:
            wd = width if kind in ("rows", "cols") else LANES // 2
            if kind.endswith("rows"):
                out_shape.append(jax.ShapeDtypeStruct((n, wd), dt))
                out_specs.append(pl.BlockSpec((tm, wd), lambda i: (i, 0)))
            else:
                out_shape.append(jax.ShapeDtypeStruct((n // seq, wd, seq), dt))
                out_specs.append(pl.BlockSpec((1, wd, tm), lambda i: (i // n_tab, 0, i % n_tab)))
    tab_spec = pl.BlockSpec((tm, LANES), lambda i: (i % n_tab, 0))
    return pl.pallas_call(
        functools.partial(_norm_proj_kernel, segs=segs),
        grid=(n // tm,),
        in_specs=[pl.BlockSpec((tm, d), lambda i: (i, 0)), _const_spec((1, d)), _const_spec(w_bf16.shape),
                  tab_spec, tab_spec, tab_spec, tab_spec],
        out_specs=out_specs, out_shape=out_shape,
        compiler_params=_cparams("parallel"), name="norm_proj",
    )(x2d, gamma.reshape(1, d), w_bf16, *tables)


def _rope_tables(pos, rows):
    half = HEAD_DIM // 2
    inv = ROPE_THETA ** (-jnp.arange(half, dtype=F32) * 2.0 / HEAD_DIM)
    ang = pos.astype(F32)[:, None] * inv[None, :]
    cos, sin = jnp.cos(ang), jnp.sin(ang)
    one, zero = jnp.ones_like(cos), jnp.zeros_like(sin)
    tabs = [jnp.concatenate([cos, cos, cos, cos], -1), jnp.concatenate([-sin, sin, -sin, sin], -1),
            jnp.concatenate([cos, cos, one, one], -1), jnp.concatenate([-sin, sin, zero, zero], -1)]
    return [jnp.broadcast_to(t, (rows, LANES)) for t in tabs]


def _rg_gate_math(uc, wa_ref, wx_ref, ba_ref, bx_ref, lam_ref):
    ub = uc.astype(BF16)
    r = _sigmoid(_dot(ub, wa_ref[...]) + ba_ref[...])
    i = _sigmoid(_dot(ub, wx_ref[...]) + bx_ref[...])
    nl = -lam_ref[...]
    softplus = jnp.maximum(nl, 0.0) + jnp.log1p(jnp.exp(-jnp.abs(nl)))
    log_a = (-RG_C * softplus) * r
    a = jnp.exp(log_a)
    th = jnp.tanh(log_a)
    b = jnp.sqrt(-2.0 * th / (1.0 - th)) * i * uc
    return a, b


def _rglru_seq_kernel(u_ref, g_ref, cw_ref, cb_ref, wa_ref, wx_ref, ba_ref, bx_ref, lam_ref,
                      y_ref, hl_ref, a_s, b_s, tail_s, h_s):
    nb, tc, w = u_ref.shape
    nslab = w // LANES

    @pl.when(pl.program_id(0) == 0)
    def _():
        tail_s[...] = jnp.zeros_like(tail_s)
        h_s[...] = jnp.zeros_like(h_s)

    for b in range(nb):
        u = u_ref[b]
        prev = tail_s[b]
        uc = (_shift_rows(u, prev, 3) * cw_ref[0:1, :] + _shift_rows(u, prev, 2) * cw_ref[1:2, :]
              + _shift_rows(u, prev, 1) * cw_ref[2:3, :] + u * cw_ref[3:4, :] + cb_ref[...])
        tail_s[b] = u[tc - SUBLANES:tc]
        a, bi = _rg_gate_math(uc, wa_ref, wx_ref, ba_ref, bx_ref, lam_ref)
        for l in range(nslab):
            a_s[l, pl.ds(b, tc, stride=nb), :] = a[:, l * LANES:(l + 1) * LANES]
            b_s[l, pl.ds(b, tc, stride=nb), :] = bi[:, l * LANES:(l + 1) * LANES]

    def step(t, hs):
        r0 = pl.multiple_of(t * nb, nb)
        new = []
        for l in range(nslab):
            h = a_s[l, pl.ds(r0, nb), :] * hs[l] + b_s[l, pl.ds(r0, nb), :]
            b_s[l, pl.ds(r0, nb), :] = h
            new.append(h)
        return tuple(new)

    hs = lax.fori_loop(0, tc, step, tuple(h_s[l] for l in range(nslab)), unroll=8)
    for l in range(nslab):
        h_s[l] = hs[l]
        hl_ref[:, l * LANES:(l + 1) * LANES] = hs[l]

    for b in range(nb):
        gate = _gelu(g_ref[b])
        for l in range(nslab):
            hb = b_s[l, pl.ds(b, tc, stride=nb), :]
            y_ref[b, :, l * LANES:(l + 1) * LANES] = (hb * gate[:, l * LANES:(l + 1) * LANES]).astype(y_ref.dtype)


def _rglru_seq(ug, cw, cb, wa_bd, wx_bd, ba, bx, lam, tc):
    nb, t, w2 = ug.shape
    w = w2 // 2
    assert nb == SUBLANES and t % tc == 0 and w % LANES == 0
    row = lambda a: a.reshape(1, w)
    return pl.pallas_call(
        _rglru_seq_kernel,
        grid=(t // tc,),
        in_specs=[pl.BlockSpec((nb, tc, w), lambda i: (0, i, 0)), pl.BlockSpec((nb, tc, w), lambda i: (0, i, 1)),
                  _const_spec(cw.shape), _const_spec((1, w)), _const_spec((w, w)), _const_spec((w, w)),
                  _const_spec((1, w)), _const_spec((1, w)), _const_spec((1, w))],
        out_specs=[pl.BlockSpec((nb, tc, w), lambda i: (0, i, 0)), pl.BlockSpec((nb, w), lambda i: (0, 0))],
        out_shape=[jax.ShapeDtypeStruct((nb, t, w), BF16), jax.ShapeDtypeStruct((nb, w), F32)],
        scratch_shapes=[pltpu.VMEM((w // LANES, tc * nb, LANES), F32), pltpu.VMEM((w // LANES, tc * nb, LANES), F32),
                        pltpu.VMEM((nb, SUBLANES, w), F32), pltpu.VMEM((w // LANES, nb, LANES), F32)],
        compiler_params=_cparams("arbitrary"), name="rglru_seq",
    )(ug, ug, cw, row(cb), wa_bd, wx_bd, row(ba), row(bx), row(lam))


def _rglru_step_kernel(u_ref, g_ref, cp_ref, h_ref, cw_ref, cb_ref, wa_ref, wx_ref, ba_ref, bx_ref, lam_ref,
                       y_ref, hn_ref):
    w = u_ref.shape[1]
    u = u_ref[...]
    uc = (cp_ref[:, 0:w] * cw_ref[0:1, :] + cp_ref[:, w:2 * w] * cw_ref[1:2, :]
          + cp_ref[:, 2 * w:3 * w] * cw_ref[2:3, :] + u * cw_ref[3:4, :] + cb_ref[...])
    a, bi = _rg_gate_math(uc, wa_ref, wx_ref, ba_ref, bx_ref, lam_ref)
    h = a * h_ref[...] + bi
    hn_ref[...] = h
    y_ref[...] = (h * _gelu(g_ref[...])).astype(y_ref.dtype)


def _rglru_step(ug, conv_prev, h_prev, cw, cb, wa_bd, wx_bd, ba, bx, lam):
    nb, w2 = ug.shape
    w = w2 // 2
    row = lambda a: a.reshape(1, w)
    full = lambda shape: pl.BlockSpec(shape, lambda i: (0,) * len(shape))
    return pl.pallas_call(
        _rglru_step_kernel,
        grid=(1,),
        in_specs=[pl.BlockSpec((nb, w), lambda i: (0, 0)), pl.BlockSpec((nb, w), lambda i: (0, 1)),
                  full((nb, 3 * w)), full((nb, w)), full(cw.shape), full((1, w)), full((w, w)), full((w, w)),
                  full((1, w)), full((1, w)), full((1, w))],
        out_specs=[full((nb, w)), full((nb, w))],
        out_shape=[jax.ShapeDtypeStruct((nb, w), BF16), jax.ShapeDtypeStruct((nb, w), F32)],
        compiler_params=_cparams("arbitrary"), name="rglru_step",
    )(ug, ug, conv_prev.reshape(nb, 3 * w), h_prev, cw, row(cb), wa_bd, wx_bd, row(ba), row(bx), row(lam))


ONES_ROWS = 16
LOG2E = 1.4426950408889634


def _flash_t(s, m_old, acc, vext):
    m_new = jnp.maximum(m_old, jnp.max(s, axis=0, keepdims=True))
    p = jnp.exp2(s - m_new).astype(BF16)
    return m_new, jnp.exp2(m_old - m_new) * acc + _dot(vext, p)


def _moba_seq_kernel(q_ref, k_ref, vt_ref, o_ref, km_s, qt_s, bias_s, acc_s, *, nblk):
    i = pl.program_id(2)
    blk = MOBA_BLOCK
    hd = HEAD_DIM

    @pl.when(i == 0)
    def _():
        for n in range(nblk):
            km_s[n:n + 1, :] = jnp.mean(k_ref[0, n * blk:(n + 1) * blk, :].astype(F32), axis=0, keepdims=True)

    qt = q_ref[0].T
    row_head = lax.broadcasted_iota(I32, (LANES, 1), 0) // hd
    blk_id = lax.broadcasted_iota(I32, (nblk, 1), 0)
    key_row = lax.broadcasted_iota(I32, (blk, 1), 0)
    q_lane = lax.broadcasted_iota(I32, (1, blk), 1)
    d0 = pl.multiple_of(i * blk, blk)
    kd = k_ref[0, pl.ds(d0, blk), :]
    ones = jnp.ones((ONES_ROWS, blk), BF16)

    ms = []
    for hh in range(2):
        qth = jnp.where(row_head == hh, qt, 0.0)
        gate = jnp.dot(km_s[...], qth, precision=_HI, preferred_element_type=F32)
        gate = jnp.where(blk_id < i, gate, -jnp.inf)
        rank = jnp.zeros(gate.shape, I32)
        for m in range(nblk):
            gm = gate[m:m + 1, :]
            rank = rank + jnp.where(gm > gate, 1, jnp.where(gm == gate, (m < blk_id).astype(I32), 0))
        bias_s[hh] = jnp.where(jnp.where(blk_id < i, rank, MOBA_TOPK) < MOBA_TOPK, 0.0, NEG)
        qb = (qth * (hd ** -0.5 * LOG2E)).astype(BF16)
        qt_s[hh] = qb
        s = jnp.where(key_row <= q_lane, _dot(kd, qb), NEG)
        m = jnp.max(s, axis=0, keepdims=True)
        p = jnp.exp2(s - m).astype(BF16)
        vext = jnp.concatenate([vt_ref[0, hh * hd:(hh + 1) * hd, pl.ds(d0, blk)], ones], axis=0)
        acc_s[hh] = _dot(vext, p)
        ms.append(m)

    def past(n, ms):
        n0 = pl.multiple_of(n * blk, blk)
        kt = k_ref[0, pl.ds(n0, blk), :]
        ss = [_dot(kt, qt_s[hh]) + bias_s[hh, pl.ds(n, 1), :] for hh in range(2)]
        out = []
        for hh in range(2):
            vext = jnp.concatenate([vt_ref[0, hh * hd:(hh + 1) * hd, pl.ds(n0, blk)], ones], axis=0)
            m_new, acc_s[hh] = _flash_t(ss[hh], ms[hh], acc_s[hh], vext)
            out.append(m_new)
        return tuple(out)

    lax.fori_loop(0, i, past, tuple(ms))
    ot = jnp.concatenate([acc_s[hh, 0:hd] / acc_s[hh, hd:hd + 1] for hh in range(2)], axis=0)
    o_ref[0] = ot.T.astype(o_ref.dtype)


def _moba_seq(q, kb, vtb):
    nb, t, hw = q.shape
    blk = MOBA_BLOCK
    assert t % blk == 0 and hw % LANES == 0 and 2 * HEAD_DIM == LANES
    nblk = t // blk
    return pl.pallas_call(
        functools.partial(_moba_seq_kernel, nblk=nblk),
        grid=(nb, hw // LANES, nblk),
        in_specs=[pl.BlockSpec((1, blk, LANES), lambda b, h, i: (b, i, h)),
                  pl.BlockSpec((1, t, LANES), lambda b, h, i: (b, 0, h)),
                  pl.BlockSpec((1, LANES, t), lambda b, h, i: (b, h, 0))],
        out_specs=pl.BlockSpec((1, blk, LANES), lambda b, h, i: (b, i, h)),
        out_shape=jax.ShapeDtypeStruct((nb, t, hw), BF16),
        scratch_shapes=[pltpu.VMEM((nblk, LANES), F32), pltpu.VMEM((2, LANES, blk), BF16),
                        pltpu.VMEM((2, nblk, blk), F32), pltpu.VMEM((2, HEAD_DIM + ONES_ROWS, blk), F32)],
        compiler_params=_cparams("parallel", "parallel", "arbitrary"), name="moba_seq",
    )(q, kb, vtb)


def _dsa_seq_kernel(qi_ref, kiw_ref, ki_ref, q_ref, k_ref, vt_ref, o_ref, key_s, qit_s, qt_s, acc_s,
                    *, tk, nsel, idx_bits):
    i = pl.program_id(1)
    tq = qi_ref.shape[1]
    nkv, hd = DSA_KV_HEADS, HEAD_DIM
    grp = q_ref.shape[2] // hd // nkv
    nck = ((i + 1) * tq + tk - 1) // tk
    qpos = i * tq + lax.broadcasted_iota(I32, (1, tq), 1)
    krow = lax.broadcasted_iota(I32, (tk, 1), 0)

    qit = (qi_ref[0] * (IDX_DIM ** -0.5)).T.astype(BF16)
    qit_s[...] = jnp.concatenate([qit[h * IDX_DIM:(h + 1) * IDX_DIM] for h in range(IDX_HEADS)], axis=1)
    wt = kiw_ref[0].T[IDX_DIM:IDX_DIM + IDX_HEADS] * (IDX_HEADS ** -0.5)
    qt = (q_ref[0] * (hd ** -0.5 * LOG2E)).T.astype(BF16)
    zero = jnp.zeros((hd, grp * tq), BF16)
    for n in range(nkv):
        own = jnp.concatenate([qt[(n * grp + g) * hd:(n * grp + g + 1) * hd] for g in range(grp)], axis=1)
        qt_s[n] = jnp.concatenate([own, zero] if n % 2 == 0 else [zero, own], axis=0)

    def score_chunk(c, carry):
        c0 = pl.multiple_of(c * tk, tk)
        d = _dot(ki_ref[0, pl.ds(c0, tk), :], qit_s[...])
        sc = jnp.zeros((tk, tq), F32)
        for h in range(IDX_HEADS):
            sc = sc + wt[h:h + 1, :] * jnp.maximum(d[:, h * tq:(h + 1) * tq], 0.0)
        sc = jnp.where(c0 + krow <= qpos, sc, -jnp.inf)
        key_s[pl.ds(c0, tk), :] = _sortable(sc)
        return carry

    lax.fori_loop(0, nck, score_chunk, 0)

    def count(pred):
        def body(c, acc):
            c0 = pl.multiple_of(c * tk, tk)
            hit = pred(key_s[pl.ds(c0, tk), :], c0)
            return acc + jnp.sum(hit.reshape(tk // SUBLANES, SUBLANES, tq), axis=0)
        acc = lax.fori_loop(0, nck, body, jnp.zeros((SUBLANES, tq), I32))
        return jnp.sum(acc, axis=0, keepdims=True)

    def value_bit(it, thr):
        cand = thr + jnp.left_shift(jnp.int32(1), 31 - it)
        cnt = count(lambda key, c0: jnp.where(key >= cand, 1, 0))
        return jnp.where(cnt >= nsel, cand, thr)

    thr = lax.fori_loop(0, 32, value_bit, jnp.full((1, tq), INT_MIN, I32))
    need = nsel - count(lambda key, c0: jnp.where(key > thr, 1, 0))
    ties = count(lambda key, c0: jnp.where(key == thr, 1, 0))

    def index_bit(it, cut):
        cand = cut + jnp.left_shift(jnp.int32(1), idx_bits - 1 - it)
        cnt = count(lambda key, c0: jnp.where(key == thr, jnp.where(c0 + krow < cand, 1, 0), 0))
        return jnp.where(cnt < need, cand, cut)

    cut = lax.cond(jnp.max(ties - need) > 0,
                   lambda: lax.fori_loop(0, idx_bits, index_bit, jnp.zeros((1, tq), I32)),
                   lambda: jnp.full((1, tq), 2 ** idx_bits, I32))

    acc_s[...] = jnp.zeros(acc_s.shape, F32)
    ones = jnp.ones((ONES_ROWS, tk), BF16)

    def attend(c, ms):
        c0 = pl.multiple_of(c * tk, tk)
        key = key_s[pl.ds(c0, tk), :]
        kpos = c0 + krow
        take = jnp.where(key > thr, 1, jnp.where(key == thr, jnp.where(kpos <= cut, 1, 0), 0))
        take = jnp.where(kpos <= qpos, take, 0)
        bias = jnp.where(take > 0, 0.0, NEG)
        bias = jnp.concatenate([bias] * grp, axis=1)
        kc = k_ref[0, pl.ds(c0, tk), :]
        scores = lambda n: _dot(kc[:, (n // 2) * LANES:(n // 2 + 1) * LANES], qt_s[n]) + bias
        s_next = scores(0)
        out = []
        for n in range(nkv):
            s = s_next
            if n + 1 < nkv:
                s_next = scores(n + 1)
            vext = jnp.concatenate([vt_ref[0, n * hd:(n + 1) * hd, pl.ds(c0, tk)], ones], axis=0)
            m_new, acc_s[n] = _flash_t(s, ms[n], acc_s[n], vext)
            out.append(m_new)
        return tuple(out)

    lax.fori_loop(0, nck, attend, tuple(jnp.full((1, grp * tq), NEG, F32) for _ in range(nkv)))
    pieces = []
    for n in range(nkv):
        on = acc_s[n, 0:hd] / acc_s[n, hd:hd + 1]
        pieces += [on[:, g * tq:(g + 1) * tq] for g in range(grp)]
    o_ref[0] = jnp.concatenate(pieces, axis=0).T.astype(o_ref.dtype)


def _dsa_seq(qi, kiw, kib, q, kb, vtb, tq, tk):
    nb, t, qw = q.shape
    assert t % tq == 0 and t % tk == 0 and tq == LANES and 2 * HEAD_DIM == LANES and DSA_KV_HEADS % 2 == 0
    nsel = min(DSA_TOPK, t // 4)
    kvw = kb.shape[2]
    grp = qw // HEAD_DIM // DSA_KV_HEADS
    return pl.pallas_call(
        functools.partial(_dsa_seq_kernel, tk=tk, nsel=nsel, idx_bits=max(1, (t - 1).bit_length())),
        grid=(nb, t // tq),
        in_specs=[pl.BlockSpec((1, tq, qi.shape[2]), lambda b, i: (b, i, 0)),
                  pl.BlockSpec((1, tq, LANES), lambda b, i: (b, i, 0)),
                  pl.BlockSpec((1, t, IDX_DIM), lambda b, i: (b, 0, 0)),
                  pl.BlockSpec((1, tq, qw), lambda b, i: (b, i, 0)),
                  pl.BlockSpec((1, t, kvw), lambda b, i: (b, 0, 0)),
                  pl.BlockSpec((1, kvw, t), lambda b, i: (b, 0, 0))],
        out_specs=pl.BlockSpec((1, tq, qw), lambda b, i: (b, i, 0)),
        out_shape=jax.ShapeDtypeStruct((nb, t, qw), BF16),
        scratch_shapes=[pltpu.VMEM((t, tq), I32), pltpu.VMEM((IDX_DIM, IDX_HEADS * tq), BF16),
                        pltpu.VMEM((DSA_KV_HEADS, LANES, grp * tq), BF16),
                        pltpu.VMEM((DSA_KV_HEADS, HEAD_DIM + ONES_ROWS, grp * tq), F32)],
        compiler_params=_cparams("parallel", "arbitrary"), name="dsa_seq",
    )(qi, kiw, kib, q, kb, vtb)


def _ffn_core(x1, gn_ref, wg_ref, wu_ref, cw_ref, cb_ref, wd_ref, conv_fn, fc):
    xn = _rms(x1, gn_ref[...]).astype(BF16)
    dff = wg_ref.shape[1]
    acc = jnp.zeros(x1.shape, F32)
    for c in range(dff // fc):
        cs = slice(c * fc, (c + 1) * fc)
        g = _dot(xn, wg_ref[:, cs])
        u = _dot(xn, wu_ref[:, cs])
        gm2, gm1 = conv_fn(g, cs)
        gc = gm2 * cw_ref[0:1, cs] + gm1 * cw_ref[1:2, cs] + g * cw_ref[2:3, cs] + cb_ref[:, cs]
        acc = acc + _dot((_gelu(gc) * u).astype(BF16), wd_ref[cs, :])
    return x1 + acc


def _ffn_seq_kernel(*refs, n_y, tiles_per_seq, final, fc):
    x_ref = refs[0]
    y_refs = refs[1:1 + n_y]
    wo_refs = refs[1 + n_y:1 + 2 * n_y]
    gn_ref, wg_ref, wu_ref, cw_ref, cb_ref, wd_ref, gf_ref = refs[1 + 2 * n_y:8 + 2 * n_y]
    out_ref, st_ref, tail_s = refs[8 + 2 * n_y:]
    tm = x_ref.shape[0]

    @pl.when(pl.program_id(0) % tiles_per_seq == 0)
    def _():
        tail_s[...] = jnp.zeros_like(tail_s)

    x1 = x_ref[...]
    for y_ref, wo_ref in zip(y_refs, wo_refs):
        x1 = x1 + _dot(y_ref[...], wo_ref[...])

    def conv_fn(g, cs):
        prev = tail_s[:, cs]
        tail_s[:, cs] = g[tm - SUBLANES:tm]
        return _shift_rows(g, prev, 2), _shift_rows(g, prev, 1)

    x2 = _ffn_core(x1, gn_ref, wg_ref, wu_ref, cw_ref, cb_ref, wd_ref, conv_fn, fc)
    st_ref[0] = tail_s[...]
    out_ref[...] = _rms(x2, gf_ref[...]) if final else x2


def _ffn_step_kernel(*refs, n_y, final, fc):
    x_ref = refs[0]
    y_refs = refs[1:1 + n_y]
    wo_refs = refs[1 + n_y:1 + 2 * n_y]
    gn_ref, wg_ref, wu_ref, cw_ref, cb_ref, wd_ref, gf_ref, p_ref = refs[1 + 2 * n_y:9 + 2 * n_y]
    out_ref, g_ref = refs[9 + 2 * n_y:]
    dff = wg_ref.shape[1]

    x1 = x_ref[...]
    for y_ref, wo_ref in zip(y_refs, wo_refs):
        x1 = x1 + _dot(y_ref[...], wo_ref[...])

    def conv_fn(g, cs):
        g_ref[:, cs] = g
        return p_ref[:, cs], p_ref[:, dff + cs.start:dff + cs.stop]

    x2 = _ffn_core(x1, gn_ref, wg_ref, wu_ref, cw_ref, cb_ref, wd_ref, conv_fn, fc)
    out_ref[...] = _rms(x2, gf_ref[...]) if final else x2


def _ffn_weights_specs(d, dff):
    return [_const_spec((1, d)), _const_spec((d, dff)), _const_spec((d, dff)), _const_spec((3, dff)),
            _const_spec((1, dff)), _const_spec((dff, d)), _const_spec((1, d))]


def _ffn_seq(x2d, ys, wos, gn, wg, wu, cw, cb, wd, gfinal, final, seq_len, tm, fc=1024):
    n, d = x2d.shape
    dff = wg.shape[1]
    assert seq_len % tm == 0 and dff % fc == 0
    tps = seq_len // tm
    row_spec = lambda c: pl.BlockSpec((tm, c), lambda i: (i, 0))
    return pl.pallas_call(
        functools.partial(_ffn_seq_kernel, n_y=len(ys), tiles_per_seq=tps, final=final, fc=fc),
        grid=(n // tm,),
        in_specs=[row_spec(d)] + [row_spec(y.shape[1]) for y in ys] + [_const_spec(w.shape) for w in wos]
                 + _ffn_weights_specs(d, dff),
        out_specs=[row_spec(d), pl.BlockSpec((1, SUBLANES, dff), lambda i: (i // tps, 0, 0))],
        out_shape=[jax.ShapeDtypeStruct((n, d), F32), jax.ShapeDtypeStruct((n // seq_len, SUBLANES, dff), F32)],
        scratch_shapes=[pltpu.VMEM((SUBLANES, dff), F32)],
        compiler_params=_cparams("arbitrary"), name="ffn_seq",
    )(x2d, *ys, *wos, gn.reshape(1, d), wg, wu, cw, cb.reshape(1, dff), wd, gfinal.reshape(1, d))


def _ffn_step(x2d, ys, wos, gn, wg, wu, cw, cb, wd, gfinal, final, prev, fc=1024):
    n, d = x2d.shape
    dff = wg.shape[1]
    full = lambda a: pl.BlockSpec(a.shape, lambda i: (0,) * a.ndim)
    prev2 = prev.reshape(n, 2 * dff)
    return pl.pallas_call(
        functools.partial(_ffn_step_kernel, n_y=len(ys), final=final, fc=fc),
        grid=(1,),
        in_specs=[full(x2d)] + [full(y) for y in ys] + [_const_spec(w.shape) for w in wos]
                 + _ffn_weights_specs(d, dff) + [full(prev2)],
        out_specs=[pl.BlockSpec((n, d), lambda i: (0, 0)), pl.BlockSpec((n, dff), lambda i: (0, 0))],
        out_shape=[jax.ShapeDtypeStruct((n, d), F32), jax.ShapeDtypeStruct((n, dff), F32)],
        compiler_params=_cparams("arbitrary"), name="ffn_step",
    )(x2d, *ys, *wos, gn.reshape(1, d), wg, wu, cw, cb.reshape(1, dff), wd, gfinal.reshape(1, d), prev2)


def _moba_pick_kernel(pt_ref, q_ref, *refs, pages_per_step):
    page_refs = refs[:pages_per_step]
    sel_ref, tv_s, ti_s = refs[pages_per_step:]
    s = pl.program_id(1)
    ppb = MOBA_BLOCK // PAGE_SIZE
    bps = pages_per_step // ppb
    heads, hd = q_ref.shape[1], q_ref.shape[2]

    @pl.when(s == 0)
    def _():
        tv_s[...] = jnp.full(tv_s.shape, -jnp.inf, F32)
        ti_s[...] = jnp.zeros(ti_s.shape, I32)

    own = (lax.broadcasted_iota(I32, (heads, heads * hd), 0) == lax.broadcasted_iota(I32, (heads, heads * hd), 1) // hd)
    qbd = jnp.where(own, jnp.concatenate([q_ref[0]] * heads, axis=1), 0.0)
    t1, t2, t3 = tv_s[0], tv_s[1], tv_s[2]
    i1, i2, i3 = ti_s[0], ti_s[1], ti_s[2]
    for j in range(bps):
        tot = jnp.zeros((heads, 1), F32)
        for r in range(ppb):
            kt = page_refs[j * ppb + r][0].reshape(heads * hd, PAGE_SIZE)
            sc = jnp.dot(qbd, kt, precision=_HI, preferred_element_type=F32)
            tot = tot + jnp.sum(sc, axis=1, keepdims=True)
        g = tot * (1.0 / MOBA_BLOCK)
        n = s * bps + j
        c1, c2, c3 = g > t1, g > t2, g > t3
        t3, i3 = jnp.where(c2, t2, jnp.where(c3, g, t3)), jnp.where(c2, i2, jnp.where(c3, n, i3))
        t2, i2 = jnp.where(c1, t1, jnp.where(c2, g, t2)), jnp.where(c1, i1, jnp.where(c2, n, i2))
        t1, i1 = jnp.where(c1, g, t1), jnp.where(c1, n, i1)
    tv_s[0], tv_s[1], tv_s[2] = t1, t2, t3
    ti_s[0], ti_s[1], ti_s[2] = i1, i2, i3

    @pl.when(s == pl.num_programs(1) - 1)
    def _():
        lane = lax.broadcasted_iota(I32, (heads, LANES), 1)
        sel_ref[0] = jnp.where(lane == 0, i1, jnp.where(lane == 1, i2, i3))


def _moba_pick(page_table, q, cache_kt, pages_per_step=16):
    nb, npg = page_table.shape
    _, heads, hd, _ = cache_kt.shape
    ppb = MOBA_BLOCK // PAGE_SIZE
    assert MOBA_TOPK == 3 and npg % pages_per_step == 0 and pages_per_step % ppb == 0 and npg // ppb >= MOBA_TOPK
    page_spec = lambda j: pl.BlockSpec((1, heads, hd, PAGE_SIZE),
                                       lambda b, s, pt: (pt[b, s * pages_per_step + j], 0, 0, 0))
    grid_spec = pltpu.PrefetchScalarGridSpec(
        num_scalar_prefetch=1, grid=(nb, npg // pages_per_step),
        in_specs=[pl.BlockSpec((1, heads, hd), lambda b, s, pt: (b, 0, 0))]
                 + [page_spec(j) for j in range(pages_per_step)],
        out_specs=pl.BlockSpec((1, heads, LANES), lambda b, s, pt: (b, 0, 0)),
        scratch_shapes=[pltpu.VMEM((MOBA_TOPK, heads, 1), F32), pltpu.VMEM((MOBA_TOPK, heads, 1), I32)])
    return pl.pallas_call(
        functools.partial(_moba_pick_kernel, pages_per_step=pages_per_step),
        grid_spec=grid_spec, out_shape=jax.ShapeDtypeStruct((nb, heads, LANES), I32),
        compiler_params=_cparams("parallel", "arbitrary"), name="moba_pick",
    )(page_table, q, *([cache_kt] * pages_per_step))


def _moba_step_kernel(sel_ref, pt_ref, q_ref, kn_ref, vn_ref, *refs, npage):
    k_refs = refs[:npage]
    v_refs = refs[npage:2 * npage]
    o_ref = refs[2 * npage]
    hd = q_ref.shape[3]
    q8 = jnp.broadcast_to(q_ref[0, 0] * (HEAD_DIM ** -0.5), (SUBLANES, hd))
    m = jnp.sum(q8 * kn_ref[0, 0], axis=1, keepdims=True)
    l = jnp.ones_like(m)
    acc = jnp.broadcast_to(vn_ref[0, 0], (SUBLANES, hd))
    for j in range(npage):
        s = _dot(q8.astype(BF16), k_refs[j][0, 0].astype(BF16))
        m_new = jnp.maximum(m, jnp.max(s, axis=1, keepdims=True))
        alpha = jnp.exp(m - m_new)
        p = jnp.exp(s - m_new)
        l = alpha * l + jnp.sum(p, axis=1, keepdims=True)
        acc = alpha * acc + _dot_t(p.astype(BF16), v_refs[j][0, 0].astype(BF16))
        m = m_new
    o_ref[0, 0] = (acc / l)[0:1]


def _moba_step(sel, page_table, q, k_new, v_new, cache_kt, cache_vt):
    nb, heads, hd = q.shape
    ppb = MOBA_BLOCK // PAGE_SIZE
    npage = MOBA_TOPK * ppb

    def page_spec(j):
        r, pg = divmod(j, ppb)
        return pl.BlockSpec((1, 1, hd, PAGE_SIZE),
                            lambda b, h, sel, pt: (pt[b, sel[b, h * MOBA_TOPK + r] * ppb + pg], h, 0, 0))

    vec = pl.BlockSpec((1, 1, 1, hd), lambda b, h, sel, pt: (b, h, 0, 0))
    grid_spec = pltpu.PrefetchScalarGridSpec(
        num_scalar_prefetch=2, grid=(nb, heads),
        in_specs=[vec, vec, vec] + [page_spec(j) for j in range(npage)] * 2,
        out_specs=vec)
    r4 = lambda a: a.reshape(nb, heads, 1, hd)
    return pl.pallas_call(
        functools.partial(_moba_step_kernel, npage=npage),
        grid_spec=grid_spec, out_shape=jax.ShapeDtypeStruct((nb, heads, 1, hd), F32),
        compiler_params=_cparams("parallel", "arbitrary"), name="moba_step",
    )(sel, page_table, r4(q), r4(k_new), r4(v_new), *([cache_kt] * npage), *([cache_vt] * npage)).reshape(nb, heads * hd)


def _dsa_score_kernel(pt_ref, qi_ref, w_ref, *refs, pages_per_step):
    page_refs = refs[:pages_per_step]
    sc_ref = refs[pages_per_step]
    qi = qi_ref[0] * (IDX_DIM ** -0.5)
    w = w_ref[0] * (IDX_HEADS ** -0.5)
    for j in range(pages_per_step):
        d = _dot(qi.astype(BF16), page_refs[j][0].astype(BF16))
        sc_ref[0, j:j + 1, :] = jnp.sum(w * jnp.maximum(d, 0.0), axis=0, keepdims=True)


def _dsa_score(page_table, qi, wi, cache_ki, pages_per_step=16):
    nb, npg = page_table.shape
    assert npg % pages_per_step == 0
    page_spec = lambda j: pl.BlockSpec((1, IDX_DIM, PAGE_SIZE), lambda b, s, pt: (pt[b, s * pages_per_step + j], 0, 0))
    grid_spec = pltpu.PrefetchScalarGridSpec(
        num_scalar_prefetch=1, grid=(nb, npg // pages_per_step),
        in_specs=[pl.BlockSpec((1, IDX_HEADS, IDX_DIM), lambda b, s, pt: (b, 0, 0)),
                  pl.BlockSpec((1, IDX_HEADS, 1), lambda b, s, pt: (b, 0, 0))]
                 + [page_spec(j) for j in range(pages_per_step)],
        out_specs=pl.BlockSpec((1, pages_per_step, PAGE_SIZE), lambda b, s, pt: (b, s, 0)))
    return pl.pallas_call(
        functools.partial(_dsa_score_kernel, pages_per_step=pages_per_step),
        grid_spec=grid_spec, out_shape=jax.ShapeDtypeStruct((nb, npg, PAGE_SIZE), F32),
        compiler_params=_cparams("parallel", "arbitrary"), name="dsa_score",
    )(page_table, qi, wi, *([cache_ki] * pages_per_step))


def _dsa_step_kernel(pt_ref, sc_ref, qi_ref, w_ref, kin_ref, q_ref, kn_ref, vn_ref, *refs,
                     pages_per_step, nsel, idx_bits):
    k_refs = refs[:pages_per_step]
    v_refs = refs[pages_per_step:2 * pages_per_step]
    o_ref, bias_s, m_s, l_s, acc_s = refs[2 * pages_per_step:]
    s = pl.program_id(1)
    nh = q_ref.shape[1]
    kvw = kn_ref.shape[2]
    grp = nh // DSA_KV_HEADS
    npg = sc_ref.shape[1]
    row_kv = lax.broadcasted_iota(I32, (nh, kvw), 0) // grp
    lane_kv = lax.broadcasted_iota(I32, (nh, kvw), 1) // HEAD_DIM
    own = row_kv == lane_kv
    q = q_ref[0] * (HEAD_DIM ** -0.5)
    qbd = jnp.where(own, jnp.concatenate([q] * DSA_KV_HEADS, axis=1), 0.0)

    @pl.when(s == 0)
    def _():
        qi = qi_ref[0] * (IDX_DIM ** -0.5)
        d = jnp.sum(qi * kin_ref[0], axis=1, keepdims=True)
        sc_new = jnp.sum(w_ref[0] * (IDX_HEADS ** -0.5) * jnp.maximum(d, 0.0), axis=0, keepdims=True)
        key = _sortable(sc_ref[0])
        key_new = _sortable(sc_new)
        pos = (lax.broadcasted_iota(I32, key.shape, 0) * PAGE_SIZE + lax.broadcasted_iota(I32, key.shape, 1))
        pos_new = npg * PAGE_SIZE

        def total(hit, hit_new):
            t = jnp.sum(jnp.sum(hit, axis=1, keepdims=True), axis=0, keepdims=True)
            return t + hit_new

        def value_bit(it, thr):
            cand = thr + jnp.left_shift(jnp.int32(1), 31 - it)
            cnt = total(jnp.where(key >= cand, 1, 0), jnp.where(key_new >= cand, 1, 0))
            return jnp.where(cnt >= nsel, cand, thr)

        thr = lax.fori_loop(0, 32, value_bit, jnp.full((1, 1), INT_MIN, I32))
        need = nsel - total(jnp.where(key > thr, 1, 0), jnp.where(key_new > thr, 1, 0))

        def index_bit(it, cut):
            cand = cut + jnp.left_shift(jnp.int32(1), idx_bits - 1 - it)
            cnt = total(jnp.where(key == thr, jnp.where(pos < cand, 1, 0), 0),
                        jnp.where(key_new == thr, jnp.where(pos_new < cand, 1, 0), 0))
            return jnp.where(cnt < need, cand, cut)

        cut = lax.fori_loop(0, idx_bits, index_bit, jnp.zeros((1, 1), I32))
        take = jnp.where(key > thr, 1, jnp.where(key == thr, jnp.where(pos <= cut, 1, 0), 0))
        bias_s[...] = jnp.where(take > 0, 0.0, NEG)
        take_new = jnp.where(key_new > thr, 1, jnp.where(key_new == thr, jnp.where(pos_new <= cut, 1, 0), 0))
        bias_new = jnp.where(take_new > 0, 0.0, NEG)
        m_s[...] = jnp.sum(qbd * kn_ref[0], axis=1, keepdims=True) + bias_new
        l_s[...] = jnp.ones_like(l_s)
        acc_s[...] = jnp.broadcast_to(vn_ref[0], acc_s.shape)

    m, l, acc = m_s[...], l_s[...], acc_s[...]
    for j in range(pages_per_step):
        kt = k_refs[j][0].reshape(kvw, PAGE_SIZE)
        vt = v_refs[j][0].reshape(kvw, PAGE_SIZE)
        sc = _dot(qbd.astype(BF16), kt.astype(BF16)) + bias_s[pl.ds(s * pages_per_step + j, 1), :]
        m_new = jnp.maximum(m, jnp.max(sc, axis=1, keepdims=True))
        alpha = jnp.exp(m - m_new)
        p = jnp.exp(sc - m_new)
        l = alpha * l + jnp.sum(p, axis=1, keepdims=True)
        acc = alpha * acc + _dot_t(p.astype(BF16), vt.astype(BF16))
        m = m_new
    m_s[...], l_s[...], acc_s[...] = m, l, acc

    @pl.when(s == pl.num_programs(1) - 1)
    def _():
        on = jnp.where(own, acc / l, 0.0)
        o = on[:, 0:HEAD_DIM]
        for n in range(1, DSA_KV_HEADS):
            o = o + on[:, n * HEAD_DIM:(n + 1) * HEAD_DIM]
        o_ref[0] = o


def _dsa_step(page_table, scores, qi, wi, ki_new, q, k_new, v_new, cache_k, cache_v, pages_per_step=16):
    nb, npg = page_table.shape
    nh = q.shape[1]
    kvw = k_new.shape[1]
    total_len = npg * PAGE_SIZE + 1
    nsel = min(DSA_TOPK, total_len // 4)
    bmap = lambda b, s, pt: (b, 0, 0)
    page_spec = lambda j: pl.BlockSpec((1, DSA_KV_HEADS, HEAD_DIM, PAGE_SIZE),
                                       lambda b, s, pt: (pt[b, s * pages_per_step + j], 0, 0, 0))
    grid_spec = pltpu.PrefetchScalarGridSpec(
        num_scalar_prefetch=1, grid=(nb, npg // pages_per_step),
        in_specs=[pl.BlockSpec((1, npg, PAGE_SIZE), bmap), pl.BlockSpec((1, IDX_HEADS, IDX_DIM), bmap),
                  pl.BlockSpec((1, IDX_HEADS, 1), bmap), pl.BlockSpec((1, 1, IDX_DIM), bmap),
                  pl.BlockSpec((1, nh, HEAD_DIM), bmap), pl.BlockSpec((1, 1, kvw), bmap), pl.BlockSpec((1, 1, kvw), bmap)]
                 + [page_spec(j) for j in range(pages_per_step)] * 2,
        out_specs=pl.BlockSpec((1, nh, HEAD_DIM), bmap),
        scratch_shapes=[pltpu.VMEM((npg, PAGE_SIZE), F32), pltpu.VMEM((nh, 1), F32), pltpu.VMEM((nh, 1), F32),
                        pltpu.VMEM((nh, kvw), F32)])
    return pl.pallas_call(
        functools.partial(_dsa_step_kernel, pages_per_step=pages_per_step, nsel=nsel,
                          idx_bits=max(1, (total_len - 1).bit_length())),
        grid_spec=grid_spec, out_shape=jax.ShapeDtypeStruct((nb, nh, HEAD_DIM), F32),
        compiler_params=_cparams("parallel", "arbitrary"), name="dsa_step",
    )(page_table, scores, qi, wi, ki_new.reshape(nb, 1, IDX_DIM), q, k_new.reshape(nb, 1, kvw),
      v_new.reshape(nb, 1, kvw), *([cache_k] * pages_per_step), *([cache_v] * pages_per_step))


def _in0_segs(w, step):
    rows = ((F32, "rows"),)
    if step:
        return ((0, 2 * w, None, 1.0, rows), (2 * w, w, "heads", 1.0, rows), (3 * w, w, "heads", 1.0, rows),
                (4 * w, w, None, 1.0, rows))
    return ((0, 2 * w, None, 1.0, rows),
            (2 * w, w, "heads", 1.0, rows),
            (3 * w, w, "heads", 1.0, ((F32, "cols"), (BF16, "rows"))),
            (4 * w, w, None, 1.0, ((F32, "cols"), (BF16, "cols"))))


def _in1_segs(qw, kvw, iw, step):
    rows = ((F32, "rows"),)
    last = qw + 2 * kvw + iw
    if step:
        return ((0, qw, "heads", 1.0, rows), (qw, kvw, "heads", 1.0, rows), (qw + kvw, kvw, None, 1.0, rows),
                (qw + 2 * kvw, iw, "heads", 1.0, rows), (last, LANES, "half", 1.0, rows + ((F32, "half_rows"),)))
    return ((0, qw, "heads", 1.0, rows),
            (qw, kvw, "heads", 1.0, ((F32, "cols"), (BF16, "rows"))),
            (qw + kvw, kvw, None, 1.0, ((F32, "cols"), (BF16, "cols"))),
            (qw + 2 * kvw, iw, "heads", 1.0, rows),
            (last, LANES, "half", 1.0, rows + ((F32, "half_cols"), (BF16, "half_rows"))))


def _prep_weights(p):
    d = p["w_in0"].shape[0]
    w = p["rg_lambda"].shape[0]
    nblocks = p["rg_gate_a_w"].shape[0]
    eye = jnp.eye(nblocks, dtype=F32)
    bd = lambda g: jnp.einsum("ncd,nm->ncmd", g, eye).reshape(w, w).astype(BF16)
    w_in1 = p["w_in1"]
    pad = (-w_in1.shape[1]) % LANES
    return dict(
        w_in0=p["w_in0"].astype(BF16), wa=bd(p["rg_gate_a_w"]), wx=bd(p["rg_gate_x_w"]),
        w_out0a=p["w_out0"][:w].astype(BF16), w_out0b=p["w_out0"][w:].astype(BF16),
        w_in1=jnp.pad(w_in1, ((0, 0), (0, pad))).astype(BF16), w_out1=p["w_out1"].astype(BF16),
        wg=p["ffn_w_gate"].astype(BF16), wu=p["ffn_w_up"].astype(BF16), wd=p["ffn_w_down"].astype(BF16))


def _prompt_group(x, p, wb):
    nb, t, d = x.shape
    n = nb * t
    w = p["rg_lambda"].shape[0]
    tm = 512
    tabs = _rope_tables(jnp.arange(t, dtype=I32), t)
    x2 = x.reshape(n, d)

    ug, q0, k0t, k0b, v0t, v0tb = _norm_proj(x2, p["norm_mix"][0], wb["w_in0"], tabs, _in0_segs(w, False), tm,
                                             t // tm)
    ug3 = ug.reshape(nb, t, 2 * w)
    y_rg, h_last = _rglru_seq(ug3, p["rg_conv_w"], p["rg_conv_b"], wb["wa"], wb["wx"], p["rg_gate_a_b"],
                              p["rg_gate_x_b"], p["rg_lambda"], tc=256)
    o0 = _moba_seq(q0.reshape(nb, t, w), k0b.reshape(nb, t, w), v0tb)
    x3, st0 = _ffn_seq(x2, [y_rg.reshape(n, w), o0.reshape(n, w)], [wb["w_out0a"], wb["w_out0b"]],
                       p["norm_ffn"][0], wb["wg"][0], wb["wu"][0], p["ffn_conv_w"][0], p["ffn_conv_b"][0],
                       wb["wd"][0], p["norm_final"], False, t, tm)

    qw = p["w_out1"].shape[0]
    kvw = DSA_KV_HEADS * HEAD_DIM
    iw = IDX_HEADS * IDX_DIM
    q1, k1t, k1b, v1t, v1tb, qi, kiw, kit, kib = _norm_proj(
        x3, p["norm_mix"][1], wb["w_in1"], tabs, _in1_segs(qw, kvw, iw, False), tm, t // tm)
    r3 = lambda a: a.reshape(nb, t, a.shape[1])
    o1 = _dsa_seq(r3(qi), r3(kiw), r3(kib), r3(q1), r3(k1b), v1tb, tq=128, tk=256)
    y, st1 = _ffn_seq(x3, [o1.reshape(n, qw)], [wb["w_out1"]], p["norm_ffn"][1], wb["wg"][1], wb["wu"][1],
                      p["ffn_conv_w"][1], p["ffn_conv_b"][1], wb["wd"][1], p["norm_final"], True, t, tm)

    heads = w // HEAD_DIM
    ffn_state = jnp.stack([st0[:, SUBLANES - 2:], st1[:, SUBLANES - 2:]])
    per_token = lambda a, h: a.reshape(nb, h, HEAD_DIM, t).transpose(0, 3, 1, 2)
    return (y.reshape(nb, t, d), per_token(k0t, heads), per_token(v0t, heads), h_last, ug3[:, t - 3:, :w],
            per_token(k1t, DSA_KV_HEADS), per_token(v1t, DSA_KV_HEADS), kit.transpose(0, 2, 1), ffn_state)


def _sample_group(x, p, wb, cache_k0, cache_v0, state_h0, state_conv0, cache_k1, cache_v1, cache_kidx1,
                  state_ffn, page_table):
    nb, t, d = x.shape
    assert t == 1
    w = p["rg_lambda"].shape[0]
    npool = cache_k0.shape[0]
    npg = page_table.shape[1]
    assert (npg * PAGE_SIZE) % MOBA_BLOCK == 0
    tabs = _rope_tables(jnp.full((1,), npg * PAGE_SIZE, I32), nb)
    x2 = x.reshape(nb, d)

    ug, q0, k0, v0 = _norm_proj(x2, p["norm_mix"][0], wb["w_in0"], tabs, _in0_segs(w, True), nb, 1)
    y_rg, h_new = _rglru_step(ug, state_conv0, state_h0, p["rg_conv_w"], p["rg_conv_b"], wb["wa"], wb["wx"],
                              p["rg_gate_a_b"], p["rg_gate_x_b"], p["rg_lambda"])
    heads = w // HEAD_DIM
    paged_t = lambda c: c.transpose(0, 2, 3, 1)
    ck0, cv0 = paged_t(cache_k0), paged_t(cache_v0)
    hsplit = lambda a: a.reshape(nb, heads, HEAD_DIM)
    picked = _moba_pick(page_table, hsplit(q0), ck0)
    sel = picked[:, :, :MOBA_TOPK].reshape(nb, heads * MOBA_TOPK)
    o0 = _moba_step(sel, page_table, hsplit(q0), hsplit(k0), hsplit(v0), ck0, cv0)
    x3, g0 = _ffn_step(x2, [y_rg, o0.astype(BF16)], [wb["w_out0a"], wb["w_out0b"]], p["norm_ffn"][0], wb["wg"][0],
                       wb["wu"][0], p["ffn_conv_w"][0], p["ffn_conv_b"][0], wb["wd"][0], p["norm_final"], False,
                       state_ffn[0])

    qw = p["w_out1"].shape[0]
    kvw = DSA_KV_HEADS * HEAD_DIM
    iw = IDX_HEADS * IDX_DIM
    q1, k1, v1, qi, kiw, ki = _norm_proj(
        x3, p["norm_mix"][1], wb["w_in1"], tabs, _in1_segs(qw, kvw, iw, True), nb, 1)
    qi3 = qi.reshape(nb, IDX_HEADS, IDX_DIM)
    wi3 = kiw[:, IDX_DIM:IDX_DIM + IDX_HEADS].reshape(nb, IDX_HEADS, 1)
    scores = _dsa_score(page_table, qi3, wi3, cache_kidx1.transpose(0, 2, 1))
    o1 = _dsa_step(page_table, scores, qi3, wi3, ki, q1.reshape(nb, qw // HEAD_DIM, HEAD_DIM), k1, v1,
                   paged_t(cache_k1), paged_t(cache_v1))
    y, g1 = _ffn_step(x3, [o1.reshape(nb, qw).astype(BF16)], [wb["w_out1"]], p["norm_ffn"][1], wb["wg"][1],
                      wb["wu"][1], p["ffn_conv_w"][1], p["ffn_conv_b"][1], wb["wd"][1], p["norm_final"], True,
                      state_ffn[1])

    conv_new = jnp.concatenate([state_conv0[:, 1:], ug[:, None, :w]], axis=1)
    ffn_state = jnp.stack([jnp.stack([state_ffn[0][:, 1], g0], axis=1), jnp.stack([state_ffn[1][:, 1], g1], axis=1)])
    return (y.reshape(nb, 1, d), k0.reshape(nb, 1, heads, HEAD_DIM), v0.reshape(nb, 1, heads, HEAD_DIM), h_new,
            conv_new, k1.reshape(nb, 1, DSA_KV_HEADS, HEAD_DIM), v1.reshape(nb, 1, DSA_KV_HEADS, HEAD_DIM),
            ki.reshape(nb, 1, IDX_DIM), ffn_state)


def kernel(x_prompt, x_sample, cache_k0, cache_v0, state_h0, state_conv0, cache_k1, cache_v1, cache_kidx1,
           state_ffn, page_table, norm_mix, norm_ffn, norm_final, w_in0, rg_conv_w, rg_conv_b, rg_gate_a_w,
           rg_gate_a_b, rg_gate_x_w, rg_gate_x_b, rg_lambda, w_out0, w_in1, w_out1, ffn_w_gate, ffn_w_up,
           ffn_conv_w, ffn_conv_b, ffn_w_down):
    p = dict(norm_mix=norm_mix, norm_ffn=norm_ffn, norm_final=norm_final, w_in0=w_in0, rg_conv_w=rg_conv_w,
             rg_conv_b=rg_conv_b, rg_gate_a_w=rg_gate_a_w, rg_gate_a_b=rg_gate_a_b, rg_gate_x_w=rg_gate_x_w,
             rg_gate_x_b=rg_gate_x_b, rg_lambda=rg_lambda, w_out0=w_out0, w_in1=w_in1, w_out1=w_out1,
             ffn_w_gate=ffn_w_gate, ffn_w_up=ffn_w_up, ffn_conv_w=ffn_conv_w, ffn_conv_b=ffn_conv_b,
             ffn_w_down=ffn_w_down)
    wb = _prep_weights(p)
    (y_p, k0_p, v0_p, h0_p, conv0_p, k1_p, v1_p, kidx1_p, ffn_p) = _prompt_group(x_prompt, p, wb)
    (y_s, k0_s, v0_s, h0_s, conv0_s, k1_s, v1_s, kidx1_s, ffn_s) = _sample_group(
        x_sample, p, wb, cache_k0, cache_v0, state_h0, state_conv0, cache_k1, cache_v1, cache_kidx1, state_ffn,
        page_table)
    return (y_p, y_s, k0_p, v0_p, h0_p, conv0_p, k1_p, v1_p, kidx1_p, ffn_p,
            k0_s, v0_s, h0_s, conv0_s, k1_s, v1_s, kidx1_s, ffn_s)
```

```python
import functools

import jax
import jax.numpy as jnp
from jax import lax
from jax.experimental import pallas as pl
from jax.experimental.pallas import tpu as pltpu

F32 = jnp.float32
BF16 = jnp.bfloat16
I32 = jnp.int32

HEAD_DIM = 64
PAGE_SIZE = 128
RG_C = 8.0
MOBA_BLOCK = 256
MOBA_TOPK = 3
DSA_KV_HEADS = 4
IDX_HEADS = 8
IDX_DIM = 64
DSA_TOPK = 256
ROPE_THETA = 10000.0
EPS = 1e-6

LANES = 128
SUBLANES = 8
VMEM_LIMIT = 56 * 1024 * 1024
NEG = -1e30
INT_MIN = -(2 ** 31)

_HI = lax.Precision.HIGHEST

ROW_TILE = 512
SCAN_TILE = 256
DSA_Q_TILE = LANES
DSA_K_TILE = 512
MOBA_HEADS_PER_STEP = 4
MOBA_QBLOCKS_PER_STEP = 2
PAGES_PER_STEP = 16


def _cparams(*sem):
    return pltpu.CompilerParams(dimension_semantics=sem, vmem_limit_bytes=VMEM_LIMIT)


def _const_spec(shape):
    nd = len(shape)
    return pl.BlockSpec(shape, lambda *_: (0,) * nd, pipeline_mode=pl.Buffered(1))


def _gelu(x):
    return x * (0.5 * (1.0 + jnp.tanh(0.7978845608028654 * (x + 0.044715 * (x * x * x)))))


def _sigmoid(x):
    return 1.0 / (1.0 + jnp.exp(-x))


def _rms(x, g):
    return x * lax.rsqrt(jnp.mean(x * x, axis=-1, keepdims=True) + EPS) * g


def _dot(a, b):
    return jnp.dot(a, b, preferred_element_type=F32)


def _dot_t(a, b, precision=None):
    return lax.dot_general(a, b, (((1,), (1,)), ((), ())), precision=precision,
                           preferred_element_type=F32)


def _rope_group(x, cos, sin_signed):
    lane = lax.broadcasted_iota(I32, x.shape, 1)
    first_half = (lane % HEAD_DIM) < (HEAD_DIM // 2)
    partner = jnp.where(first_half, pltpu.roll(x, LANES - HEAD_DIM // 2, 1), pltpu.roll(x, HEAD_DIM // 2, 1))
    return x * cos + partner * sin_signed


def _shift_rows(x, prev, j):
    r = pltpu.roll(x, j, 0)
    row = lax.broadcasted_iota(I32, (SUBLANES, x.shape[1]), 0)
    head = jnp.where(row < j, pltpu.roll(prev, j, 0), r[0:SUBLANES])
    if x.shape[0] == SUBLANES:
        return head
    return jnp.concatenate([head, r[SUBLANES:]], axis=0)


def _sortable(x):
    b = pltpu.bitcast(x, I32)
    return b ^ ((b >> 31) & 0x7FFFFFFF)


def _norm_proj_kernel(x_ref, g_ref, w_ref, cos_ref, sin_ref, cosk_ref, sink_ref, *out_refs, segs):
    xb = _rms(x_ref[...], g_ref[...]).astype(BF16)
    oi = 0
    for c0, width, rope, scale, outs in segs:
        for g0 in range(0, width, LANES):
            y = _dot(xb, w_ref[:, c0 + g0:c0 + g0 + LANES])
            if rope == "heads":
                y = _rope_group(y, cos_ref[...], sin_ref[...])
            elif rope == "half":
                y = _rope_group(y, cosk_ref[...], sink_ref[...])
            if scale != 1.0:
                y = y * scale
            yt = y.T if any(kind.endswith("cols") for _, kind in outs) else None
            for k, (dt, kind) in enumerate(outs):
                o_ref = out_refs[oi + k]
                if kind == "rows":
                    o_ref[:, g0:g0 + LANES] = y.astype(dt)
                elif kind == "cols":
                    o_ref[0, g0:g0 + LANES, :] = yt.astype(dt)
                elif kind == "half_rows":
                    o_ref[...] = y[:, :LANES // 2].astype(dt)
                else:
                    o_ref[0] = yt[:LANES // 2].astype(dt)
        oi += len(outs)


def _norm_proj(x2d, gamma, w_bf16, tables, segs, tm, n_tab):
    n, d = x2d.shape
    seq = n_tab * tm
    out_shape, out_specs = [], []
    for c0, width, rope, scale, outs in segs:
        for dt, kind in outs:
            wd = width if kind in ("rows", "cols") else LANES // 2
            if kind.endswith("rows"):
                out_shape.append(jax.ShapeDtypeStruct((n, wd), dt))
                out_specs.append(pl.BlockSpec((tm, wd), lambda i: (i, 0)))
            else:
                out_shape.append(jax.ShapeDtypeStruct((n // seq, wd, seq), dt))
                out_specs.append(pl.BlockSpec((1, wd, tm), lambda i: (i // n_tab, 0, i % n_tab)))
    tab_spec = pl.BlockSpec((tm, LANES), lambda i: (i % n_tab, 0))
    return pl.pallas_call(
        functools.partial(_norm_proj_kernel, segs=segs),
        grid=(n // tm,),
        in_specs=[pl.BlockSpec((tm, d), lambda i: (i, 0)), _const_spec((1, d)), _const_spec(w_bf16.shape),
                  tab_spec, tab_spec, tab_spec, tab_spec],
        out_specs=out_specs, out_shape=out_shape,
        compiler_params=_cparams("parallel"), name="norm_proj",
    )(x2d, gamma.reshape(1, d), w_bf16, *tables)


def _rope_tables(pos, rows):
    half = HEAD_DIM // 2
    inv = ROPE_THETA ** (-jnp.arange(half, dtype=F32) * 2.0 / HEAD_DIM)
    ang = pos.astype(F32)[:, None] * inv[None, :]
    cos, sin = jnp.cos(ang), jnp.sin(ang)
    one, zero = jnp.ones_like(cos), jnp.zeros_like(sin)
    tabs = [jnp.concatenate([cos, cos, cos, cos], -1), jnp.concatenate([-sin, sin, -sin, sin], -1),
            jnp.concatenate([cos, cos, one, one], -1), jnp.concatenate([-sin, sin, zero, zero], -1)]
    return [jnp.broadcast_to(t, (rows, LANES)) for t in tabs]


def _rg_gate_math(uc, wa_ref, wx_ref, ba_ref, bx_ref, lam_ref):
    ub = uc.astype(BF16)
    r = _sigmoid(_dot(ub, wa_ref[...]) + ba_ref[...])
    i = _sigmoid(_dot(ub, wx_ref[...]) + bx_ref[...])
    nl = -lam_ref[...]
    softplus = jnp.maximum(nl, 0.0) + jnp.log1p(jnp.exp(-jnp.abs(nl)))
    log_a = (-RG_C * softplus) * r
    a = jnp.exp(log_a)
    th = jnp.tanh(log_a)
    b = jnp.sqrt(-2.0 * th / (1.0 - th)) * i * uc
    return a, b


def _rglru_seq_kernel(u_ref, g_ref, cw_ref, cb_ref, wa_ref, wx_ref, ba_ref, bx_ref, lam_ref,
                      y_ref, hl_ref, a_s, b_s, tail_s, h_s):
    nb, tc, w = u_ref.shape
    nslab = w // LANES

    @pl.when(pl.program_id(0) == 0)
    def _():
        tail_s[...] = jnp.zeros_like(tail_s)
        h_s[...] = jnp.zeros_like(h_s)

    for b in range(nb):
        u = u_ref[b]
        prev = tail_s[b]
        uc = (_shift_rows(u, prev, 3) * cw_ref[0:1, :] + _shift_rows(u, prev, 2) * cw_ref[1:2, :]
              + _shift_rows(u, prev, 1) * cw_ref[2:3, :] + u * cw_ref[3:4, :] + cb_ref[...])
        tail_s[b] = u[tc - SUBLANES:tc]
        a, bi = _rg_gate_math(uc, wa_ref, wx_ref, ba_ref, bx_ref, lam_ref)
        for l in range(nslab):
            a_s[l, pl.ds(b, tc, stride=nb), :] = a[:, l * LANES:(l + 1) * LANES]
            b_s[l, pl.ds(b, tc, stride=nb), :] = bi[:, l * LANES:(l + 1) * LANES]

    def step(t, hs):
        r0 = pl.multiple_of(t * nb, nb)
        new = []
        for l in range(nslab):
            h = a_s[l, pl.ds(r0, nb), :] * hs[l] + b_s[l, pl.ds(r0, nb), :]
            b_s[l, pl.ds(r0, nb), :] = h
            new.append(h)
        return tuple(new)

    hs = lax.fori_loop(0, tc, step, tuple(h_s[l] for l in range(nslab)), unroll=8)
    for l in range(nslab):
        h_s[l] = hs[l]
        hl_ref[:, l * LANES:(l + 1) * LANES] = hs[l]

    for b in range(nb):
        gate = _gelu(g_ref[b])
        for l in range(nslab):
            hb = b_s[l, pl.ds(b, tc, stride=nb), :]
            y_ref[b, :, l * LANES:(l + 1) * LANES] = (hb * gate[:, l * LANES:(l + 1) * LANES]).astype(y_ref.dtype)


def _rglru_seq(ug, cw, cb, wa_bd, wx_bd, ba, bx, lam, tc):
    nb, t, w2 = ug.shape
    w = w2 // 2
    assert nb == SUBLANES and t % tc == 0 and w % LANES == 0
    row = lambda a: a.reshape(1, w)
    return pl.pallas_call(
        _rglru_seq_kernel,
        grid=(t // tc,),
        in_specs=[pl.BlockSpec((nb, tc, w), lambda i: (0, i, 0)), pl.BlockSpec((nb, tc, w), lambda i: (0, i, 1)),
                  _const_spec(cw.shape), _const_spec((1, w)), _const_spec((w, w)), _const_spec((w, w)),
                  _const_spec((1, w)), _const_spec((1, w)), _const_spec((1, w))],
        out_specs=[pl.BlockSpec((nb, tc, w), lambda i: (0, i, 0)), pl.BlockSpec((nb, w), lambda i: (0, 0))],
        out_shape=[jax.ShapeDtypeStruct((nb, t, w), BF16), jax.ShapeDtypeStruct((nb, w), F32)],
        scratch_shapes=[pltpu.VMEM((w // LANES, tc * nb, LANES), F32), pltpu.VMEM((w // LANES, tc * nb, LANES), F32),
                        pltpu.VMEM((nb, SUBLANES, w), F32), pltpu.VMEM((w // LANES, nb, LANES), F32)],
        compiler_params=_cparams("arbitrary"), name="rglru_seq",
    )(ug, ug, cw, row(cb), wa_bd, wx_bd, row(ba), row(bx), row(lam))


def _rglru_step_kernel(u_ref, g_ref, cp_ref, h_ref, cw_ref, cb_ref, wa_ref, wx_ref, ba_ref, bx_ref, lam_ref,
                       y_ref, hn_ref):
    w = u_ref.shape[1]
    u = u_ref[...]
    uc = (cp_ref[:, 0:w] * cw_ref[0:1, :] + cp_ref[:, w:2 * w] * cw_ref[1:2, :]
          + cp_ref[:, 2 * w:3 * w] * cw_ref[2:3, :] + u * cw_ref[3:4, :] + cb_ref[...])
    a, bi = _rg_gate_math(uc, wa_ref, wx_ref, ba_ref, bx_ref, lam_ref)
    h = a * h_ref[...] + bi
    hn_ref[...] = h
    y_ref[...] = (h * _gelu(g_ref[...])).astype(y_ref.dtype)


def _rglru_step(ug, conv_prev, h_prev, cw, cb, wa_bd, wx_bd, ba, bx, lam):
    nb, w2 = ug.shape
    w = w2 // 2
    row = lambda a: a.reshape(1, w)
    full = lambda shape: pl.BlockSpec(shape, lambda i: (0,) * len(shape))
    return pl.pallas_call(
        _rglru_step_kernel,
        grid=(1,),
        in_specs=[pl.BlockSpec((nb, w), lambda i: (0, 0)), pl.BlockSpec((nb, w), lambda i: (0, 1)),
                  full((nb, 3 * w)), full((nb, w)), full(cw.shape), full((1, w)), full((w, w)), full((w, w)),
                  full((1, w)), full((1, w)), full((1, w))],
        out_specs=[full((nb, w)), full((nb, w))],
        out_shape=[jax.ShapeDtypeStruct((nb, w), BF16), jax.ShapeDtypeStruct((nb, w), F32)],
        compiler_params=_cparams("arbitrary"), name="rglru_step",
    )(ug, ug, conv_prev.reshape(nb, 3 * w), h_prev, cw, row(cb), wa_bd, wx_bd, row(ba), row(bx), row(lam))


ONES_ROWS = 16
LOG2E = 1.4426950408889634


def _flash_t(s, m_old, acc, vext):
    m_new = jnp.maximum(m_old, jnp.max(s, axis=0, keepdims=True))
    p = jnp.exp2(s - m_new).astype(BF16)
    return m_new, jnp.exp2(m_old - m_new) * acc + _dot(vext, p)


def _moba_seq_kernel(q_ref, k_ref, vt_ref, o_ref, km_s, qt_s, bias_s, acc_s, *, nblk):
    i = pl.program_id(2)
    blk = MOBA_BLOCK
    hd = HEAD_DIM

    @pl.when(i == 0)
    def _():
        for n in range(nblk):
            km_s[n:n + 1, :] = jnp.mean(k_ref[0, n * blk:(n + 1) * blk, :].astype(F32), axis=0, keepdims=True)

    tq = q_ref.shape[1]
    nqb = tq // blk
    hps = q_ref.shape[2] // hd
    qt = q_ref[0].T
    row_head = lax.broadcasted_iota(I32, (LANES, 1), 0) // hd
    blk_id = lax.broadcasted_iota(I32, (nblk, 1), 0)
    own_blk = i * nqb + lax.broadcasted_iota(I32, (1, tq), 1) // blk
    causal = jnp.where(lax.broadcasted_iota(I32, (blk, 1), 0) <= lax.broadcasted_iota(I32, (1, blk), 1), 0.0, NEG)
    ones = jnp.ones((ONES_ROWS, blk), BF16)
    pair = lambda a, hh: a[:, (hh // 2) * LANES:(hh // 2 + 1) * LANES]

    for hh in range(hps):
        qth = jnp.where(row_head == hh % 2, qt[(hh // 2) * LANES:(hh // 2 + 1) * LANES], 0.0)
        gate = jnp.dot(pair(km_s[...], hh), qth, precision=_HI, preferred_element_type=F32)
        gate = jnp.where(blk_id < own_blk, gate, -jnp.inf)
        rank = jnp.zeros(gate.shape, I32)
        for m in range(nblk):
            gm = gate[m:m + 1, :]
            rank = rank + jnp.where(gm > gate, 1, jnp.where(gm == gate, (m < blk_id).astype(I32), 0))
        bias_s[hh] = jnp.where(jnp.where(blk_id < own_blk, rank, MOBA_TOPK) < MOBA_TOPK, 0.0, NEG)
        qt_s[hh] = (qth * (hd ** -0.5 * LOG2E)).astype(BF16)
        acc_s[hh] = jnp.zeros(acc_s.shape[1:], F32)

    def key_block(n, ms, mask_of):
        n0 = pl.multiple_of(n * blk, blk)
        kt = k_ref[0, pl.ds(n0, blk), :]
        scores = lambda hh: _dot(pair(kt, hh), qt_s[hh]) + mask_of(hh)
        s_next = scores(0)
        out = []
        for hh in range(hps):
            s = s_next
            if hh + 1 < hps:
                s_next = scores(hh + 1)
            vext = jnp.concatenate([vt_ref[0, hh * hd:(hh + 1) * hd, pl.ds(n0, blk)], ones], axis=0)
            m_new, acc_s[hh] = _flash_t(s, ms[hh], acc_s[hh], vext)
            out.append(m_new)
        return tuple(out)

    ms = tuple(jnp.full((1, tq), NEG, F32) for _ in range(hps))
    ms = lax.fori_loop(0, i * nqb, lambda n, ms: key_block(n, ms, lambda hh: bias_s[hh, pl.ds(n, 1), :]), ms)
    for c in range(nqb):
        n = i * nqb + c

        def mask_of(hh, c=c, n=n):
            row = bias_s[hh, pl.ds(n, 1), :]
            parts = [jnp.full((blk, blk), NEG, F32) if a < c else causal if a == c else
                     jnp.broadcast_to(row[:, a * blk:(a + 1) * blk], (blk, blk)) for a in range(nqb)]
            return jnp.concatenate(parts, axis=1)

        ms = key_block(n, ms, mask_of)
    ot = jnp.concatenate([acc_s[hh, 0:hd] / acc_s[hh, hd:hd + 1] for hh in range(hps)], axis=0)
    o_ref[0] = ot.T.astype(o_ref.dtype)


def _moba_seq(q, kb, vtb):
    nb, t, hw = q.shape
    blk = MOBA_BLOCK
    hps, nqb = MOBA_HEADS_PER_STEP, MOBA_QBLOCKS_PER_STEP
    gw, tq = hps * HEAD_DIM, nqb * blk
    assert t % tq == 0 and hw % gw == 0 and 2 * HEAD_DIM == LANES and hps % 2 == 0
    nblk = t // blk
    return pl.pallas_call(
        functools.partial(_moba_seq_kernel, nblk=nblk),
        grid=(nb, hw // gw, t // tq),
        in_specs=[pl.BlockSpec((1, tq, gw), lambda b, h, i: (b, i, h)),
                  pl.BlockSpec((1, t, gw), lambda b, h, i: (b, 0, h)),
                  pl.BlockSpec((1, gw, t), lambda b, h, i: (b, h, 0))],
        out_specs=pl.BlockSpec((1, tq, gw), lambda b, h, i: (b, i, h)),
        out_shape=jax.ShapeDtypeStruct((nb, t, hw), BF16),
        scratch_shapes=[pltpu.VMEM((nblk, gw), F32), pltpu.VMEM((hps, LANES, tq), BF16),
                        pltpu.VMEM((hps, nblk, tq), F32), pltpu.VMEM((hps, HEAD_DIM + ONES_ROWS, tq), F32)],
        compiler_params=_cparams("parallel", "parallel", "arbitrary"), name="moba_seq",
    )(q, kb, vtb)


def _dsa_seq_kernel(qi_ref, kiw_ref, ki_ref, q_ref, k_ref, vt_ref, o_ref, key_s, qit_s, qt_s, acc_s,
                    *, tk, nsel, idx_bits):
    i = pl.program_id(1)
    tq = qi_ref.shape[1]
    nkv, hd = DSA_KV_HEADS, HEAD_DIM
    grp = q_ref.shape[2] // hd // nkv
    nck = ((i + 1) * tq + tk - 1) // tk
    qpos = i * tq + lax.broadcasted_iota(I32, (1, tq), 1)
    krow = lax.broadcasted_iota(I32, (tk, 1), 0)

    qit = (qi_ref[0] * (IDX_DIM ** -0.5)).T.astype(BF16)
    qit_s[...] = jnp.concatenate([qit[h * IDX_DIM:(h + 1) * IDX_DIM] for h in range(IDX_HEADS)], axis=1)
    wt = kiw_ref[0].T[IDX_DIM:IDX_DIM + IDX_HEADS] * (IDX_HEADS ** -0.5)
    qt = (q_ref[0] * (hd ** -0.5 * LOG2E)).T.astype(BF16)
    zero = jnp.zeros((hd, grp * tq), BF16)
    for n in range(nkv):
        own = jnp.concatenate([qt[(n * grp + g) * hd:(n * grp + g + 1) * hd] for g in range(grp)], axis=1)
        qt_s[n] = jnp.concatenate([own, zero] if n % 2 == 0 else [zero, own], axis=0)

    def score_chunk(c, carry):
        c0 = pl.multiple_of(c * tk, tk)
        d = _dot(ki_ref[0, pl.ds(c0, tk), :], qit_s[...])
        sc = jnp.zeros((tk, tq), F32)
        for h in range(IDX_HEADS):
            sc = sc + wt[h:h + 1, :] * jnp.maximum(d[:, h * tq:(h + 1) * tq], 0.0)
        sc = jnp.where(c0 + krow <= qpos, sc, -jnp.inf)
        key_s[pl.ds(c0, tk), :] = _sortable(sc)
        return carry

    lax.fori_loop(0, nck, score_chunk, 0)

    def count(pred):
        def body(c, acc):
            c0 = pl.multiple_of(c * tk, tk)
            hit = pred(key_s[pl.ds(c0, tk), :], c0)
            return acc + jnp.sum(hit.reshape(tk // SUBLANES, SUBLANES, tq), axis=0)
        acc = lax.fori_loop(0, nck, body, jnp.zeros((SUBLANES, tq), I32))
        return jnp.sum(acc, axis=0, keepdims=True)

    def value_bit(it, thr):
        cand = thr + jnp.left_shift(jnp.int32(1), 31 - it)
        cnt = count(lambda key, c0: jnp.where(key >= cand, 1, 0))
        return jnp.where(cnt >= nsel, cand, thr)

    thr = lax.fori_loop(0, 32, value_bit, jnp.full((1, tq), INT_MIN, I32))
    need = nsel - count(lambda key, c0: jnp.where(key > thr, 1, 0))
    ties = count(lambda key, c0: jnp.where(key == thr, 1, 0))

    def index_bit(it, cut):
        cand = cut + jnp.left_shift(jnp.int32(1), idx_bits - 1 - it)
        cnt = count(lambda key, c0: jnp.where(key == thr, jnp.where(c0 + krow < cand, 1, 0), 0))
        return jnp.where(cnt < need, cand, cut)

    cut = lax.cond(jnp.max(ties - need) > 0,
                   lambda: lax.fori_loop(0, idx_bits, index_bit, jnp.zeros((1, tq), I32)),
                   lambda: jnp.full((1, tq), 2 ** idx_bits, I32))

    acc_s[...] = jnp.zeros(acc_s.shape, F32)
    ones = jnp.ones((ONES_ROWS, tk), BF16)

    def attend(c, ms):
        c0 = pl.multiple_of(c * tk, tk)
        key = key_s[pl.ds(c0, tk), :]
        kpos = c0 + krow
        take = jnp.where(key > thr, 1, jnp.where(key == thr, jnp.where(kpos <= cut, 1, 0), 0))
        take = jnp.where(kpos <= qpos, take, 0)
        bias = jnp.where(take > 0, 0.0, NEG)
        bias = jnp.concatenate([bias] * grp, axis=1)
        kc = k_ref[0, pl.ds(c0, tk), :]
        scores = lambda n: _dot(kc[:, (n // 2) * LANES:(n // 2 + 1) * LANES], qt_s[n]) + bias
        s_next = scores(0)
        out = []
        for n in range(nkv):
            s = s_next
            if n + 1 < nkv:
                s_next = scores(n + 1)
            vext = jnp.concatenate([vt_ref[0, n * hd:(n + 1) * hd, pl.ds(c0, tk)], ones], axis=0)
            m_new, acc_s[n] = _flash_t(s, ms[n], acc_s[n], vext)
            out.append(m_new)
        return tuple(out)

    lax.fori_loop(0, nck, attend, tuple(jnp.full((1, grp * tq), NEG, F32) for _ in range(nkv)))
    pieces = []
    for n in range(nkv):
        on = acc_s[n, 0:hd] / acc_s[n, hd:hd + 1]
        pieces += [on[:, g * tq:(g + 1) * tq] for g in range(grp)]
    o_ref[0] = jnp.concatenate(pieces, axis=0).T.astype(o_ref.dtype)


def _dsa_seq(qi, kiw, kib, q, kb, vtb, tq, tk):
    nb, t, qw = q.shape
    assert t % tq == 0 and t % tk == 0 and tq == LANES and 2 * HEAD_DIM == LANES and DSA_KV_HEADS % 2 == 0
    nsel = min(DSA_TOPK, t // 4)
    kvw = kb.shape[2]
    grp = qw // HEAD_DIM // DSA_KV_HEADS
    return pl.pallas_call(
        functools.partial(_dsa_seq_kernel, tk=tk, nsel=nsel, idx_bits=max(1, (t - 1).bit_length())),
        grid=(nb, t // tq),
        in_specs=[pl.BlockSpec((1, tq, qi.shape[2]), lambda b, i: (b, i, 0)),
                  pl.BlockSpec((1, tq, LANES), lambda b, i: (b, i, 0)),
                  pl.BlockSpec((1, t, IDX_DIM), lambda b, i: (b, 0, 0)),
                  pl.BlockSpec((1, tq, qw), lambda b, i: (b, i, 0)),
                  pl.BlockSpec((1, t, kvw), lambda b, i: (b, 0, 0)),
                  pl.BlockSpec((1, kvw, t), lambda b, i: (b, 0, 0))],
        out_specs=pl.BlockSpec((1, tq, qw), lambda b, i: (b, i, 0)),
        out_shape=jax.ShapeDtypeStruct((nb, t, qw), BF16),
        scratch_shapes=[pltpu.VMEM((t, tq), I32), pltpu.VMEM((IDX_DIM, IDX_HEADS * tq), BF16),
                        pltpu.VMEM((DSA_KV_HEADS, LANES, grp * tq), BF16),
                        pltpu.VMEM((DSA_KV_HEADS, HEAD_DIM + ONES_ROWS, grp * tq), F32)],
        compiler_params=_cparams("parallel", "arbitrary"), name="dsa_seq",
    )(qi, kiw, kib, q, kb, vtb)


def _ffn_core(x1, gn_ref, wg_ref, wu_ref, cw_ref, cb_ref, wd_ref, conv_fn, fc):
    xn = _rms(x1, gn_ref[...]).astype(BF16)
    dff = wg_ref.shape[1]
    acc = jnp.zeros(x1.shape, F32)
    for c in range(dff // fc):
        cs = slice(c * fc, (c + 1) * fc)
        g = _dot(xn, wg_ref[:, cs])
        u = _dot(xn, wu_ref[:, cs])
        gm2, gm1 = conv_fn(g, cs)
        gc = gm2 * cw_ref[0:1, cs] + gm1 * cw_ref[1:2, cs] + g * cw_ref[2:3, cs] + cb_ref[:, cs]
        acc = acc + _dot((_gelu(gc) * u).astype(BF16), wd_ref[cs, :])
    return x1 + acc


def _ffn_seq_kernel(*refs, n_y, tiles_per_seq, final, fc):
    x_ref = refs[0]
    y_refs = refs[1:1 + n_y]
    wo_refs = refs[1 + n_y:1 + 2 * n_y]
    gn_ref, wg_ref, wu_ref, cw_ref, cb_ref, wd_ref, gf_ref = refs[1 + 2 * n_y:8 + 2 * n_y]
    out_ref, st_ref, tail_s = refs[8 + 2 * n_y:]
    tm = x_ref.shape[0]

    @pl.when(pl.program_id(0) % tiles_per_seq == 0)
    def _():
        tail_s[...] = jnp.zeros_like(tail_s)

    x1 = x_ref[...]
    for y_ref, wo_ref in zip(y_refs, wo_refs):
        x1 = x1 + _dot(y_ref[...], wo_ref[...])

    def conv_fn(g, cs):
        prev = tail_s[:, cs]
        tail_s[:, cs] = g[tm - SUBLANES:tm]
        return _shift_rows(g, prev, 2), _shift_rows(g, prev, 1)

    x2 = _ffn_core(x1, gn_ref, wg_ref, wu_ref, cw_ref, cb_ref, wd_ref, conv_fn, fc)
    st_ref[0] = tail_s[...]
    out_ref[...] = _rms(x2, gf_ref[...]) if final else x2


def _ffn_step_kernel(*refs, n_y, final, fc):
    x_ref = refs[0]
    y_refs = refs[1:1 + n_y]
    wo_refs = refs[1 + n_y:1 + 2 * n_y]
    gn_ref, wg_ref, wu_ref, cw_ref, cb_ref, wd_ref, gf_ref, p_ref = refs[1 + 2 * n_y:9 + 2 * n_y]
    out_ref, g_ref = refs[9 + 2 * n_y:]
    dff = wg_ref.shape[1]

    x1 = x_ref[...]
    for y_ref, wo_ref in zip(y_refs, wo_refs):
        x1 = x1 + _dot(y_ref[...], wo_ref[...])

    def conv_fn(g, cs):
        g_ref[:, cs] = g
        return p_ref[:, cs], p_ref[:, dff + cs.start:dff + cs.stop]

    x2 = _ffn_core(x1, gn_ref, wg_ref, wu_ref, cw_ref, cb_ref, wd_ref, conv_fn, fc)
    out_ref[...] = _rms(x2, gf_ref[...]) if final else x2


def _ffn_weights_specs(d, dff):
    return [_const_spec((1, d)), _const_spec((d, dff)), _const_spec((d, dff)), _const_spec((3, dff)),
            _const_spec((1, dff)), _const_spec((dff, d)), _const_spec((1, d))]


def _ffn_seq(x2d, ys, wos, gn, wg, wu, cw, cb, wd, gfinal, final, seq_len, tm, fc=1024):
    n, d = x2d.shape
    dff = wg.shape[1]
    assert seq_len % tm == 0 and dff % fc == 0
    tps = seq_len // tm
    row_spec = lambda c: pl.BlockSpec((tm, c), lambda i: (i, 0))
    return pl.pallas_call(
        functools.partial(_ffn_seq_kernel, n_y=len(ys), tiles_per_seq=tps, final=final, fc=fc),
        grid=(n // tm,),
        in_specs=[row_spec(d)] + [row_spec(y.shape[1]) for y in ys] + [_const_spec(w.shape) for w in wos]
                 + _ffn_weights_specs(d, dff),
        out_specs=[row_spec(d), pl.BlockSpec((1, SUBLANES, dff), lambda i: (i // tps, 0, 0))],
        out_shape=[jax.ShapeDtypeStruct((n, d), F32), jax.ShapeDtypeStruct((n // seq_len, SUBLANES, dff), F32)],
        scratch_shapes=[pltpu.VMEM((SUBLANES, dff), F32)],
        compiler_params=_cparams("arbitrary"), name="ffn_seq",
    )(x2d, *ys, *wos, gn.reshape(1, d), wg, wu, cw, cb.reshape(1, dff), wd, gfinal.reshape(1, d))


def _ffn_step(x2d, ys, wos, gn, wg, wu, cw, cb, wd, gfinal, final, prev, fc=1024):
    n, d = x2d.shape
    dff = wg.shape[1]
    full = lambda a: pl.BlockSpec(a.shape, lambda i: (0,) * a.ndim)
    prev2 = prev.reshape(n, 2 * dff)
    return pl.pallas_call(
        functools.partial(_ffn_step_kernel, n_y=len(ys), final=final, fc=fc),
        grid=(1,),
        in_specs=[full(x2d)] + [full(y) for y in ys] + [_const_spec(w.shape) for w in wos]
                 + _ffn_weights_specs(d, dff) + [full(prev2)],
        out_specs=[pl.BlockSpec((n, d), lambda i: (0, 0)), pl.BlockSpec((n, dff), lambda i: (0, 0))],
        out_shape=[jax.ShapeDtypeStruct((n, d), F32), jax.ShapeDtypeStruct((n, dff), F32)],
        compiler_params=_cparams("arbitrary"), name="ffn_step",
    )(x2d, *ys, *wos, gn.reshape(1, d), wg, wu, cw, cb.reshape(1, dff), wd, gfinal.reshape(1, d), prev2)


def _moba_pick_kernel(pt_ref, q_ref, *refs, pages_per_step):
    page_refs = refs[:pages_per_step]
    sel_ref, tv_s, ti_s = refs[pages_per_step:]
    s = pl.program_id(1)
    ppb = MOBA_BLOCK // PAGE_SIZE
    bps = pages_per_step // ppb
    heads = q_ref.shape[1]

    @pl.when(s == 0)
    def _():
        tv_s[...] = jnp.full(tv_s.shape, -jnp.inf, F32)
        ti_s[...] = jnp.zeros(ti_s.shape, I32)

    q = q_ref[0]
    t1, t2, t3 = tv_s[0], tv_s[1], tv_s[2]
    i1, i2, i3 = ti_s[0], ti_s[1], ti_s[2]
    for j in range(bps):
        tot = jnp.zeros((heads, PAGE_SIZE), F32)
        for r in range(ppb):
            tot = tot + jnp.sum(page_refs[j * ppb + r][0] * q, axis=1)
        g = jnp.sum(tot, axis=1, keepdims=True) * (1.0 / MOBA_BLOCK)
        n = s * bps + j
        c1, c2, c3 = g > t1, g > t2, g > t3
        t3, i3 = jnp.where(c2, t2, jnp.where(c3, g, t3)), jnp.where(c2, i2, jnp.where(c3, n, i3))
        t2, i2 = jnp.where(c1, t1, jnp.where(c2, g, t2)), jnp.where(c1, i1, jnp.where(c2, n, i2))
        t1, i1 = jnp.where(c1, g, t1), jnp.where(c1, n, i1)
    tv_s[0], tv_s[1], tv_s[2] = t1, t2, t3
    ti_s[0], ti_s[1], ti_s[2] = i1, i2, i3

    @pl.when(s == pl.num_programs(1) - 1)
    def _():
        lane = lax.broadcasted_iota(I32, (heads, LANES), 1)
        sel_ref[0] = jnp.where(lane == 0, i1, jnp.where(lane == 1, i2, i3))


def _moba_pick(page_table, q, cache_kt, pages_per_step=PAGES_PER_STEP):
    nb, npg = page_table.shape
    _, heads, hd, _ = cache_kt.shape
    ppb = MOBA_BLOCK // PAGE_SIZE
    assert MOBA_TOPK == 3 and npg % pages_per_step == 0 and pages_per_step % ppb == 0 and npg // ppb >= MOBA_TOPK
    page_spec = lambda j: pl.BlockSpec((1, heads, hd, PAGE_SIZE),
                                       lambda b, s, pt: (pt[b, s * pages_per_step + j], 0, 0, 0))
    grid_spec = pltpu.PrefetchScalarGridSpec(
        num_scalar_prefetch=1, grid=(nb, npg // pages_per_step),
        in_specs=[pl.BlockSpec((1, heads, hd, 1), lambda b, s, pt: (b, 0, 0, 0))]
                 + [page_spec(j) for j in range(pages_per_step)],
        out_specs=pl.BlockSpec((1, heads, LANES), lambda b, s, pt: (b, 0, 0)),
        scratch_shapes=[pltpu.VMEM((MOBA_TOPK, heads, 1), F32), pltpu.VMEM((MOBA_TOPK, heads, 1), I32)])
    return pl.pallas_call(
        functools.partial(_moba_pick_kernel, pages_per_step=pages_per_step),
        grid_spec=grid_spec, out_shape=jax.ShapeDtypeStruct((nb, heads, LANES), I32),
        compiler_params=_cparams("parallel", "arbitrary"), name="moba_pick",
    )(page_table, q.reshape(nb, heads, hd, 1), *([cache_kt] * pages_per_step))


def _moba_step_kernel(sel_ref, pt_ref, q_ref, kn_ref, vn_ref, *refs, npage):
    k_refs = refs[:npage]
    v_refs = refs[npage:2 * npage]
    o_ref = refs[2 * npage]
    hd = q_ref.shape[3]
    q8 = jnp.broadcast_to(q_ref[0, 0] * (HEAD_DIM ** -0.5), (SUBLANES, hd))
    s_new = jnp.sum(q8 * kn_ref[0, 0], axis=1, keepdims=True)
    qb = q8.astype(BF16)
    s = jnp.concatenate([_dot(qb, k_refs[j][0, 0].astype(BF16)) for j in range(npage)], axis=1)
    m = jnp.maximum(s_new, jnp.max(s, axis=1, keepdims=True))
    p_new = jnp.exp(s_new - m)
    p = jnp.exp(s - m)
    l = p_new + jnp.sum(p, axis=1, keepdims=True)
    pb = p.astype(BF16)
    acc = p_new * vn_ref[0, 0]
    for j in range(npage):
        acc = acc + _dot_t(pb[:, j * PAGE_SIZE:(j + 1) * PAGE_SIZE], v_refs[j][0, 0].astype(BF16))
    o_ref[0, 0] = (acc / l)[0:1]


def _moba_step(sel, page_table, q, k_new, v_new, cache_kt, cache_vt):
    nb, heads, hd = q.shape
    ppb = MOBA_BLOCK // PAGE_SIZE
    npage = MOBA_TOPK * ppb

    def page_spec(j):
        r, pg = divmod(j, ppb)
        return pl.BlockSpec((1, 1, hd, PAGE_SIZE),
                            lambda b, h, sel, pt: (pt[b, sel[b, h * MOBA_TOPK + r] * ppb + pg], h, 0, 0))

    vec = pl.BlockSpec((1, 1, 1, hd), lambda b, h, sel, pt: (b, h, 0, 0))
    grid_spec = pltpu.PrefetchScalarGridSpec(
        num_scalar_prefetch=2, grid=(nb, heads),
        in_specs=[vec, vec, vec] + [page_spec(j) for j in range(npage)] * 2,
        out_specs=vec)
    r4 = lambda a: a.reshape(nb, heads, 1, hd)
    return pl.pallas_call(
        functools.partial(_moba_step_kernel, npage=npage),
        grid_spec=grid_spec, out_shape=jax.ShapeDtypeStruct((nb, heads, 1, hd), F32),
        compiler_params=_cparams("parallel", "arbitrary"), name="moba_step",
    )(sel, page_table, r4(q), r4(k_new), r4(v_new), *([cache_kt] * npage), *([cache_vt] * npage)).reshape(nb, heads * hd)


def _dsa_score_kernel(pt_ref, qi_ref, w_ref, *refs, pages_per_step):
    page_refs = refs[:pages_per_step]
    sc_ref = refs[pages_per_step]
    qi = qi_ref[0] * (IDX_DIM ** -0.5)
    w = w_ref[0] * (IDX_HEADS ** -0.5)
    for j in range(pages_per_step):
        d = _dot(qi.astype(BF16), page_refs[j][0].astype(BF16))
        sc_ref[0, j:j + 1, :] = jnp.sum(w * jnp.maximum(d, 0.0), axis=0, keepdims=True)


def _dsa_score(page_table, qi, wi, cache_ki, pages_per_step=PAGES_PER_STEP):
    nb, npg = page_table.shape
    assert npg % pages_per_step == 0
    page_spec = lambda j: pl.BlockSpec((1, IDX_DIM, PAGE_SIZE), lambda b, s, pt: (pt[b, s * pages_per_step + j], 0, 0))
    grid_spec = pltpu.PrefetchScalarGridSpec(
        num_scalar_prefetch=1, grid=(nb, npg // pages_per_step),
        in_specs=[pl.BlockSpec((1, IDX_HEADS, IDX_DIM), lambda b, s, pt: (b, 0, 0)),
                  pl.BlockSpec((1, IDX_HEADS, 1), lambda b, s, pt: (b, 0, 0))]
                 + [page_spec(j) for j in range(pages_per_step)],
        out_specs=pl.BlockSpec((1, pages_per_step, PAGE_SIZE), lambda b, s, pt: (b, s, 0)))
    return pl.pallas_call(
        functools.partial(_dsa_score_kernel, pages_per_step=pages_per_step),
        grid_spec=grid_spec, out_shape=jax.ShapeDtypeStruct((nb, npg, PAGE_SIZE), F32),
        compiler_params=_cparams("parallel", "arbitrary"), name="dsa_score",
    )(page_table, qi, wi, *([cache_ki] * pages_per_step))


def _dsa_pick_kernel(sc_ref, qi_ref, w_ref, kin_ref, bias_ref, bnew_ref, *, nsel, idx_bits):
    nb, npg, _ = sc_ref.shape
    d = jnp.sum(qi_ref[...] * (IDX_DIM ** -0.5) * kin_ref[...], axis=1, keepdims=True)
    per_head = (w_ref[...] * (IDX_HEADS ** -0.5) * jnp.maximum(d, 0.0)).reshape(nb, IDX_HEADS, 1)
    key_new = _sortable(jnp.sum(per_head, axis=1, keepdims=True))
    key = _sortable(sc_ref[...])
    pos = (lax.broadcasted_iota(I32, key.shape, 1) * PAGE_SIZE + lax.broadcasted_iota(I32, key.shape, 2))
    pos_new = npg * PAGE_SIZE

    def total(hit, hit_new):
        return jnp.sum(jnp.sum(hit, axis=1, keepdims=True), axis=2, keepdims=True) + hit_new

    def value_bit(it, thr):
        cand = thr + jnp.left_shift(jnp.int32(1), 31 - it)
        cnt = total(jnp.where(key >= cand, 1, 0), jnp.where(key_new >= cand, 1, 0))
        return jnp.where(cnt >= nsel, cand, thr)

    thr = lax.fori_loop(0, 32, value_bit, jnp.full((nb, 1, 1), INT_MIN, I32))
    need = nsel - total(jnp.where(key > thr, 1, 0), jnp.where(key_new > thr, 1, 0))

    def index_bit(it, cut):
        cand = cut + jnp.left_shift(jnp.int32(1), idx_bits - 1 - it)
        cnt = total(jnp.where(key == thr, jnp.where(pos < cand, 1, 0), 0),
                    jnp.where(key_new == thr, jnp.where(pos_new < cand, 1, 0), 0))
        return jnp.where(cnt < need, cand, cut)

    cut = lax.fori_loop(0, idx_bits, index_bit, jnp.zeros((nb, 1, 1), I32))
    take = jnp.where(key > thr, 1, jnp.where(key == thr, jnp.where(pos <= cut, 1, 0), 0))
    bias_ref[...] = jnp.where(take > 0, 0.0, NEG)
    take_new = jnp.where(key_new > thr, 1, jnp.where(key_new == thr, jnp.where(pos_new <= cut, 1, 0), 0))
    bnew_ref[...] = jnp.broadcast_to(jnp.where(take_new > 0, 0.0, NEG), bnew_ref.shape)


def _dsa_pick(scores, qi, wi, ki_new):
    nb, npg, _ = scores.shape
    total_len = npg * PAGE_SIZE + 1
    rows = nb * IDX_HEADS
    full = lambda shape: pl.BlockSpec(shape, lambda i: (0,) * len(shape))
    return pl.pallas_call(
        functools.partial(_dsa_pick_kernel, nsel=min(DSA_TOPK, total_len // 4),
                          idx_bits=max(1, (total_len - 1).bit_length())),
        grid=(1,),
        in_specs=[full(scores.shape), full((rows, IDX_DIM)), full((rows, 1)), full((rows, IDX_DIM))],
        out_specs=[full(scores.shape), full((nb, 1, LANES))],
        out_shape=[jax.ShapeDtypeStruct(scores.shape, F32), jax.ShapeDtypeStruct((nb, 1, LANES), F32)],
        compiler_params=_cparams("arbitrary"), name="dsa_pick",
    )(scores, qi.reshape(rows, IDX_DIM), wi.reshape(rows, 1), jnp.repeat(ki_new, IDX_HEADS, axis=0))


def _dsa_step_kernel(pt_ref, bias_ref, bnew_ref, q_ref, kn_ref, vn_ref, *refs, pages_per_step):
    k_refs = refs[:pages_per_step]
    v_refs = refs[pages_per_step:2 * pages_per_step]
    o_ref, m_s, l_s, acc_s = refs[2 * pages_per_step:]
    s = pl.program_id(1)
    nh = q_ref.shape[1]
    kvw = kn_ref.shape[2]
    grp = nh // DSA_KV_HEADS
    row_kv = lax.broadcasted_iota(I32, (nh, kvw), 0) // grp
    lane_kv = lax.broadcasted_iota(I32, (nh, kvw), 1) // HEAD_DIM
    own = row_kv == lane_kv
    q = q_ref[0] * (HEAD_DIM ** -0.5)
    qbd = jnp.where(own, jnp.concatenate([q] * DSA_KV_HEADS, axis=1), 0.0)

    @pl.when(s == 0)
    def _():
        m_s[...] = jnp.sum(qbd * kn_ref[0], axis=1, keepdims=True) + bnew_ref[0][:, 0:1]
        l_s[...] = jnp.ones_like(l_s)
        acc_s[...] = jnp.broadcast_to(vn_ref[0], acc_s.shape)

    m, l, acc = m_s[...], l_s[...], acc_s[...]
    qb = qbd.astype(BF16)
    sc = jnp.concatenate(
        [_dot(qb, k_refs[j][0].reshape(kvw, PAGE_SIZE).astype(BF16)) + bias_ref[0, pl.ds(s * pages_per_step + j, 1), :]
         for j in range(pages_per_step)], axis=1)
    m_new = jnp.maximum(m, jnp.max(sc, axis=1, keepdims=True))
    alpha = jnp.exp(m - m_new)
    p = jnp.exp(sc - m_new)
    l = alpha * l + jnp.sum(p, axis=1, keepdims=True)
    pb = p.astype(BF16)
    acc = alpha * acc
    for j in range(pages_per_step):
        vt = v_refs[j][0].reshape(kvw, PAGE_SIZE).astype(BF16)
        acc = acc + _dot_t(pb[:, j * PAGE_SIZE:(j + 1) * PAGE_SIZE], vt)
    m = m_new
    m_s[...], l_s[...], acc_s[...] = m, l, acc

    @pl.when(s == pl.num_programs(1) - 1)
    def _():
        on = jnp.where(own, acc / l, 0.0)
        o = on[:, 0:HEAD_DIM]
        for n in range(1, DSA_KV_HEADS):
            o = o + on[:, n * HEAD_DIM:(n + 1) * HEAD_DIM]
        o_ref[0] = o


def _dsa_step(page_table, bias, bias_new, q, k_new, v_new, cache_k, cache_v, pages_per_step=PAGES_PER_STEP):
    nb, npg = page_table.shape
    nh = q.shape[1]
    kvw = k_new.shape[1]
    bmap = lambda b, s, pt: (b, 0, 0)
    page_spec = lambda j: pl.BlockSpec((1, DSA_KV_HEADS, HEAD_DIM, PAGE_SIZE),
                                       lambda b, s, pt: (pt[b, s * pages_per_step + j], 0, 0, 0))
    grid_spec = pltpu.PrefetchScalarGridSpec(
        num_scalar_prefetch=1, grid=(nb, npg // pages_per_step),
        in_specs=[pl.BlockSpec((1, npg, PAGE_SIZE), bmap), pl.BlockSpec((1, 1, LANES), bmap),
                  pl.BlockSpec((1, nh, HEAD_DIM), bmap), pl.BlockSpec((1, 1, kvw), bmap), pl.BlockSpec((1, 1, kvw), bmap)]
                 + [page_spec(j) for j in range(pages_per_step)] * 2,
        out_specs=pl.BlockSpec((1, nh, HEAD_DIM), bmap),
        scratch_shapes=[pltpu.VMEM((nh, 1), F32), pltpu.VMEM((nh, 1), F32), pltpu.VMEM((nh, kvw), F32)])
    return pl.pallas_call(
        functools.partial(_dsa_step_kernel, pages_per_step=pages_per_step),
        grid_spec=grid_spec, out_shape=jax.ShapeDtypeStruct((nb, nh, HEAD_DIM), F32),
        compiler_params=_cparams("parallel", "arbitrary"), name="dsa_step",
    )(page_table, bias, bias_new, q, k_new.reshape(nb, 1, kvw), v_new.reshape(nb, 1, kvw),
      *([cache_k] * pages_per_step), *([cache_v] * pages_per_step))


def _in0_segs(w, step):
    rows = ((F32, "rows"),)
    if step:
        return ((0, 2 * w, None, 1.0, rows), (2 * w, w, "heads", 1.0, rows), (3 * w, w, "heads", 1.0, rows),
                (4 * w, w, None, 1.0, rows))
    return ((0, 2 * w, None, 1.0, rows),
            (2 * w, w, "heads", 1.0, rows),
            (3 * w, w, "heads", 1.0, ((F32, "cols"), (BF16, "rows"))),
            (4 * w, w, None, 1.0, ((F32, "cols"), (BF16, "cols"))))


def _in1_segs(qw, kvw, iw, step):
    rows = ((F32, "rows"),)
    last = qw + 2 * kvw + iw
    if step:
        return ((0, qw, "heads", 1.0, rows), (qw, kvw, "heads", 1.0, rows), (qw + kvw, kvw, None, 1.0, rows),
                (qw + 2 * kvw, iw, "heads", 1.0, rows), (last, LANES, "half", 1.0, rows + ((F32, "half_rows"),)))
    return ((0, qw, "heads", 1.0, rows),
            (qw, kvw, "heads", 1.0, ((F32, "cols"), (BF16, "rows"))),
            (qw + kvw, kvw, None, 1.0, ((F32, "cols"), (BF16, "cols"))),
            (qw + 2 * kvw, iw, "heads", 1.0, rows),
            (last, LANES, "half", 1.0, rows + ((F32, "half_cols"), (BF16, "half_rows"))))


def _prep_weights(p):
    d = p["w_in0"].shape[0]
    w = p["rg_lambda"].shape[0]
    nblocks = p["rg_gate_a_w"].shape[0]
    eye = jnp.eye(nblocks, dtype=F32)
    bd = lambda g: jnp.einsum("ncd,nm->ncmd", g, eye).reshape(w, w).astype(BF16)
    w_in1 = p["w_in1"]
    pad = (-w_in1.shape[1]) % LANES
    return dict(
        w_in0=p["w_in0"].astype(BF16), wa=bd(p["rg_gate_a_w"]), wx=bd(p["rg_gate_x_w"]),
        w_out0a=p["w_out0"][:w].astype(BF16), w_out0b=p["w_out0"][w:].astype(BF16),
        w_in1=jnp.pad(w_in1, ((0, 0), (0, pad))).astype(BF16), w_out1=p["w_out1"].astype(BF16),
        wg=p["ffn_w_gate"].astype(BF16), wu=p["ffn_w_up"].astype(BF16), wd=p["ffn_w_down"].astype(BF16))


def _prompt_group(x, p, wb):
    nb, t, d = x.shape
    n = nb * t
    w = p["rg_lambda"].shape[0]
    tm = ROW_TILE
    tabs = _rope_tables(jnp.arange(t, dtype=I32), t)
    x2 = x.reshape(n, d)

    ug, q0, k0t, k0b, v0t, v0tb = _norm_proj(x2, p["norm_mix"][0], wb["w_in0"], tabs, _in0_segs(w, False), tm,
                                             t // tm)
    ug3 = ug.reshape(nb, t, 2 * w)
    y_rg, h_last = _rglru_seq(ug3, p["rg_conv_w"], p["rg_conv_b"], wb["wa"], wb["wx"], p["rg_gate_a_b"],
                              p["rg_gate_x_b"], p["rg_lambda"], tc=SCAN_TILE)
    o0 = _moba_seq(q0.reshape(nb, t, w), k0b.reshape(nb, t, w), v0tb)
    x3, st0 = _ffn_seq(x2, [y_rg.reshape(n, w), o0.reshape(n, w)], [wb["w_out0a"], wb["w_out0b"]],
                       p["norm_ffn"][0], wb["wg"][0], wb["wu"][0], p["ffn_conv_w"][0], p["ffn_conv_b"][0],
                       wb["wd"][0], p["norm_final"], False, t, tm)

    qw = p["w_out1"].shape[0]
    kvw = DSA_KV_HEADS * HEAD_DIM
    iw = IDX_HEADS * IDX_DIM
    q1, k1t, k1b, v1t, v1tb, qi, kiw, kit, kib = _norm_proj(
        x3, p["norm_mix"][1], wb["w_in1"], tabs, _in1_segs(qw, kvw, iw, False), tm, t // tm)
    r3 = lambda a: a.reshape(nb, t, a.shape[1])
    o1 = _dsa_seq(r3(qi), r3(kiw), r3(kib), r3(q1), r3(k1b), v1tb, tq=DSA_Q_TILE, tk=DSA_K_TILE)
    y, st1 = _ffn_seq(x3, [o1.reshape(n, qw)], [wb["w_out1"]], p["norm_ffn"][1], wb["wg"][1], wb["wu"][1],
                      p["ffn_conv_w"][1], p["ffn_conv_b"][1], wb["wd"][1], p["norm_final"], True, t, tm)

    heads = w // HEAD_DIM
    ffn_state = jnp.stack([st0[:, SUBLANES - 2:], st1[:, SUBLANES - 2:]])
    per_token = lambda a, h: a.reshape(nb, h, HEAD_DIM, t).transpose(0, 3, 1, 2)
    return (y.reshape(nb, t, d), per_token(k0t, heads), per_token(v0t, heads), h_last, ug3[:, t - 3:, :w],
            per_token(k1t, DSA_KV_HEADS), per_token(v1t, DSA_KV_HEADS), kit.transpose(0, 2, 1), ffn_state)


def _sample_group(x, p, wb, cache_k0, cache_v0, state_h0, state_conv0, cache_k1, cache_v1, cache_kidx1,
                  state_ffn, page_table):
    nb, t, d = x.shape
    assert t == 1
    w = p["rg_lambda"].shape[0]
    npg = page_table.shape[1]
    assert (npg * PAGE_SIZE) % MOBA_BLOCK == 0
    tabs = _rope_tables(jnp.full((1,), npg * PAGE_SIZE, I32), nb)
    x2 = x.reshape(nb, d)

    ug, q0, k0, v0 = _norm_proj(x2, p["norm_mix"][0], wb["w_in0"], tabs, _in0_segs(w, True), nb, 1)
    y_rg, h_new = _rglru_step(ug, state_conv0, state_h0, p["rg_conv_w"], p["rg_conv_b"], wb["wa"], wb["wx"],
                              p["rg_gate_a_b"], p["rg_gate_x_b"], p["rg_lambda"])
    heads = w // HEAD_DIM
    paged_t = lambda c: c.transpose(0, 2, 3, 1)
    ck0, cv0 = paged_t(cache_k0), paged_t(cache_v0)
    hsplit = lambda a: a.reshape(nb, heads, HEAD_DIM)
    picked = _moba_pick(page_table, hsplit(q0), ck0)
    sel = picked[:, :, :MOBA_TOPK].reshape(nb, heads * MOBA_TOPK)
    o0 = _moba_step(sel, page_table, hsplit(q0), hsplit(k0), hsplit(v0), ck0, cv0)
    x3, g0 = _ffn_step(x2, [y_rg, o0.astype(BF16)], [wb["w_out0a"], wb["w_out0b"]], p["norm_ffn"][0], wb["wg"][0],
                       wb["wu"][0], p["ffn_conv_w"][0], p["ffn_conv_b"][0], wb["wd"][0], p["norm_final"], False,
                       state_ffn[0])

    qw = p["w_out1"].shape[0]
    kvw = DSA_KV_HEADS * HEAD_DIM
    iw = IDX_HEADS * IDX_DIM
    q1, k1, v1, qi, kiw, ki = _norm_proj(
        x3, p["norm_mix"][1], wb["w_in1"], tabs, _in1_segs(qw, kvw, iw, True), nb, 1)
    qi3 = qi.reshape(nb, IDX_HEADS, IDX_DIM)
    wi3 = kiw[:, IDX_DIM:IDX_DIM + IDX_HEADS].reshape(nb, IDX_HEADS, 1)
    scores = _dsa_score(page_table, qi3, wi3, cache_kidx1.transpose(0, 2, 1))
    bias, bias_new = _dsa_pick(scores, qi3, wi3, ki)
    o1 = _dsa_step(page_table, bias, bias_new, q1.reshape(nb, qw // HEAD_DIM, HEAD_DIM), k1, v1,
                   paged_t(cache_k1), paged_t(cache_v1))
    y, g1 = _ffn_step(x3, [o1.reshape(nb, qw).astype(BF16)], [wb["w_out1"]], p["norm_ffn"][1], wb["wg"][1],
                      wb["wu"][1], p["ffn_conv_w"][1], p["ffn_conv_b"][1], wb["wd"][1], p["norm_final"], True,
                      state_ffn[1])

    conv_new = jnp.concatenate([state_conv0[:, 1:], ug[:, None, :w]], axis=1)
    ffn_state = jnp.stack([jnp.stack([state_ffn[0][:, 1], g0], axis=1), jnp.stack([state_ffn[1][:, 1], g1], axis=1)])
    return (y.reshape(nb, 1, d), k0.reshape(nb, 1, heads, HEAD_DIM), v0.reshape(nb, 1, heads, HEAD_DIM), h_new,
            conv_new, k1.reshape(nb, 1, DSA_KV_HEADS, HEAD_DIM), v1.reshape(nb, 1, DSA_KV_HEADS, HEAD_DIM),
            ki.reshape(nb, 1, IDX_DIM), ffn_state)


def kernel(x_prompt, x_sample, cache_k0, cache_v0, state_h0, state_conv0, cache_k1, cache_v1, cache_kidx1,
           state_ffn, page_table, norm_mix, norm_ffn, norm_final, w_in0, rg_conv_w, rg_conv_b, rg_gate_a_w,
           rg_gate_a_b, rg_gate_x_w, rg_gate_x_b, rg_lambda, w_out0, w_in1, w_out1, ffn_w_gate, ffn_w_up,
           ffn_conv_w, ffn_conv_b, ffn_w_down):
    p = dict(norm_mix=norm_mix, norm_ffn=norm_ffn, norm_final=norm_final, w_in0=w_in0, rg_conv_w=rg_conv_w,
             rg_conv_b=rg_conv_b, rg_gate_a_w=rg_gate_a_w, rg_gate_a_b=rg_gate_a_b, rg_gate_x_w=rg_gate_x_w,
             rg_gate_x_b=rg_gate_x_b, rg_lambda=rg_lambda, w_out0=w_out0, w_in1=w_in1, w_out1=w_out1,
             ffn_w_gate=ffn_w_gate, ffn_w_up=ffn_w_up, ffn_conv_w=ffn_conv_w, ffn_conv_b=ffn_conv_b,
             ffn_w_down=ffn_w_down)
    wb = _prep_weights(p)
    (y_p, k0_p, v0_p, h0_p, conv0_p, k1_p, v1_p, kidx1_p, ffn_p) = _prompt_group(x_prompt, p, wb)
    (y_s, k0_s, v0_s, h0_s, conv0_s, k1_s, v1_s, kidx1_s, ffn_s) = _sample_group(
        x_sample, p, wb, cache_k0, cache_v0, state_h0, state_conv0, cache_k1, cache_v1, cache_kidx1, state_ffn,
        page_table)
    return (y_p, y_s, k0_p, v0_p, h0_p, conv0_p, k1_p, v1_p, kidx1_p, ffn_p,
            k0_s, v0_s, h0_s, conv0_s, k1_s, v1_s, kidx1_s, ffn_s)
```

```python
import functools

import jax
import jax.numpy as jnp
from jax import lax
from jax.experimental import pallas as pl
from jax.experimental.pallas import tpu as pltpu

F32 = jnp.float32
BF16 = jnp.bfloat16
I32 = jnp.int32

HEAD_DIM = 64
PAGE_SIZE = 128
RG_C = 8.0
MOBA_BLOCK = 256
MOBA_TOPK = 3
DSA_KV_HEADS = 4
IDX_HEADS = 8
IDX_DIM = 64
DSA_TOPK = 256
ROPE_THETA = 10000.0
EPS = 1e-6

LANES = 128
SUBLANES = 8
VMEM_LIMIT = 56 * 1024 * 1024
NEG = -1e30
INT_MIN = -(2 ** 31)

_HI = lax.Precision.HIGHEST

ROW_TILE = 512
SCAN_TILE = 256
DSA_Q_TILE = LANES
DSA_K_TILE = 512
MOBA_HEADS_PER_STEP = 4
MOBA_QBLOCKS_PER_STEP = 2
PAGES_PER_STEP = 32


def _cparams(*sem):
    return pltpu.CompilerParams(dimension_semantics=sem, vmem_limit_bytes=VMEM_LIMIT)


def _const_spec(shape):
    nd = len(shape)
    return pl.BlockSpec(shape, lambda *_: (0,) * nd, pipeline_mode=pl.Buffered(1))


def _gelu(x):
    return x * (0.5 * (1.0 + jnp.tanh(0.7978845608028654 * (x + 0.044715 * (x * x * x)))))


def _sigmoid(x):
    return 1.0 / (1.0 + jnp.exp(-x))


def _rms(x, g):
    return x * lax.rsqrt(jnp.mean(x * x, axis=-1, keepdims=True) + EPS) * g


def _dot(a, b):
    return jnp.dot(a, b, preferred_element_type=F32)


def _dot_t(a, b, precision=None):
    return lax.dot_general(a, b, (((1,), (1,)), ((), ())), precision=precision,
                           preferred_element_type=F32)


def _rope_group(x, cos, sin_signed):
    lane = lax.broadcasted_iota(I32, x.shape, 1)
    first_half = (lane % HEAD_DIM) < (HEAD_DIM // 2)
    partner = jnp.where(first_half, pltpu.roll(x, LANES - HEAD_DIM // 2, 1), pltpu.roll(x, HEAD_DIM // 2, 1))
    return x * cos + partner * sin_signed


def _shift_rows(x, prev, j):
    r = pltpu.roll(x, j, 0)
    row = lax.broadcasted_iota(I32, (SUBLANES, x.shape[1]), 0)
    head = jnp.where(row < j, pltpu.roll(prev, j, 0), r[0:SUBLANES])
    if x.shape[0] == SUBLANES:
        return head
    return jnp.concatenate([head, r[SUBLANES:]], axis=0)


def _sortable(x):
    b = pltpu.bitcast(x, I32)
    return b ^ ((b >> 31) & 0x7FFFFFFF)


def _norm_proj_kernel(x_ref, g_ref, w_ref, cos_ref, sin_ref, cosk_ref, sink_ref, *out_refs, segs):
    xb = _rms(x_ref[...], g_ref[...]).astype(BF16)
    oi = 0
    for c0, width, rope, scale, outs in segs:
        for g0 in range(0, width, LANES):
            y = _dot(xb, w_ref[:, c0 + g0:c0 + g0 + LANES])
            if rope == "heads":
                y = _rope_group(y, cos_ref[...], sin_ref[...])
            elif rope == "half":
                y = _rope_group(y, cosk_ref[...], sink_ref[...])
            if scale != 1.0:
                y = y * scale
            yt = y.T if any(kind.endswith("cols") for _, kind in outs) else None
            for k, (dt, kind) in enumerate(outs):
                o_ref = out_refs[oi + k]
                if kind == "rows":
                    o_ref[:, g0:g0 + LANES] = y.astype(dt)
                elif kind == "cols":
                    o_ref[0, g0:g0 + LANES, :] = yt.astype(dt)
                elif kind == "half_rows":
                    o_ref[...] = y[:, :LANES // 2].astype(dt)
                else:
                    o_ref[0] = yt[:LANES // 2].astype(dt)
        oi += len(outs)


def _norm_proj(x2d, gamma, w_bf16, tables, segs, tm, n_tab):
    n, d = x2d.shape
    seq = n_tab * tm
    out_shape, out_specs = [], []
    for c0, width, rope, scale, outs in segs:
        for dt, kind in outs:
            wd = width if kind in ("rows", "cols") else LANES // 2
            if kind.endswith("rows"):
                out_shape.append(jax.ShapeDtypeStruct((n, wd), dt))
                out_specs.append(pl.BlockSpec((tm, wd), lambda i: (i, 0)))
            else:
                out_shape.append(jax.ShapeDtypeStruct((n // seq, wd, seq), dt))
                out_specs.append(pl.BlockSpec((1, wd, tm), lambda i: (i // n_tab, 0, i % n_tab)))
    tab_spec = pl.BlockSpec((tm, LANES), lambda i: (i % n_tab, 0))
    return pl.pallas_call(
        functools.partial(_norm_proj_kernel, segs=segs),
        grid=(n // tm,),
        in_specs=[pl.BlockSpec((tm, d), lambda i: (i, 0)), _const_spec((1, d)), _const_spec(w_bf16.shape),
                  tab_spec, tab_spec, tab_spec, tab_spec],
        out_specs=out_specs, out_shape=out_shape,
        compiler_params=_cparams("parallel"), name="norm_proj",
    )(x2d, gamma.reshape(1, d), w_bf16, *tables)


def _rope_tables(pos, rows):
    half = HEAD_DIM // 2
    inv = ROPE_THETA ** (-jnp.arange(half, dtype=F32) * 2.0 / HEAD_DIM)
    ang = pos.astype(F32)[:, None] * inv[None, :]
    cos, sin = jnp.cos(ang), jnp.sin(ang)
    one, zero = jnp.ones_like(cos), jnp.zeros_like(sin)
    tabs = [jnp.concatenate([cos, cos, cos, cos], -1), jnp.concatenate([-sin, sin, -sin, sin], -1),
            jnp.concatenate([cos, cos, one, one], -1), jnp.concatenate([-sin, sin, zero, zero], -1)]
    return [jnp.broadcast_to(t, (rows, LANES)) for t in tabs]


def _rg_gate_math(uc, wa_ref, wx_ref, ba_ref, bx_ref, lam_ref):
    ub = uc.astype(BF16)
    r = _sigmoid(_dot(ub, wa_ref[...]) + ba_ref[...])
    i = _sigmoid(_dot(ub, wx_ref[...]) + bx_ref[...])
    nl = -lam_ref[...]
    softplus = jnp.maximum(nl, 0.0) + jnp.log1p(jnp.exp(-jnp.abs(nl)))
    log_a = (-RG_C * softplus) * r
    a = jnp.exp(log_a)
    th = jnp.tanh(log_a)
    b = jnp.sqrt(-2.0 * th / (1.0 - th)) * i * uc
    return a, b


def _rglru_seq_kernel(u_ref, g_ref, cw_ref, cb_ref, wa_ref, wx_ref, ba_ref, bx_ref, lam_ref,
                      y_ref, hl_ref, a_s, b_s, tail_s, h_s):
    nb, tc, w = u_ref.shape
    nslab = w // LANES

    @pl.when(pl.program_id(0) == 0)
    def _():
        tail_s[...] = jnp.zeros_like(tail_s)
        h_s[...] = jnp.zeros_like(h_s)

    for b in range(nb):
        u = u_ref[b]
        prev = tail_s[b]
        uc = (_shift_rows(u, prev, 3) * cw_ref[0:1, :] + _shift_rows(u, prev, 2) * cw_ref[1:2, :]
              + _shift_rows(u, prev, 1) * cw_ref[2:3, :] + u * cw_ref[3:4, :] + cb_ref[...])
        tail_s[b] = u[tc - SUBLANES:tc]
        a, bi = _rg_gate_math(uc, wa_ref, wx_ref, ba_ref, bx_ref, lam_ref)
        for l in range(nslab):
            a_s[l, pl.ds(b, tc, stride=nb), :] = a[:, l * LANES:(l + 1) * LANES]
            b_s[l, pl.ds(b, tc, stride=nb), :] = bi[:, l * LANES:(l + 1) * LANES]

    def step(t, hs):
        r0 = pl.multiple_of(t * nb, nb)
        new = []
        for l in range(nslab):
            h = a_s[l, pl.ds(r0, nb), :] * hs[l] + b_s[l, pl.ds(r0, nb), :]
            b_s[l, pl.ds(r0, nb), :] = h
            new.append(h)
        return tuple(new)

    hs = lax.fori_loop(0, tc, step, tuple(h_s[l] for l in range(nslab)), unroll=8)
    for l in range(nslab):
        h_s[l] = hs[l]
        hl_ref[:, l * LANES:(l + 1) * LANES] = hs[l]

    for b in range(nb):
        gate = _gelu(g_ref[b])
        for l in range(nslab):
            hb = b_s[l, pl.ds(b, tc, stride=nb), :]
            y_ref[b, :, l * LANES:(l + 1) * LANES] = (hb * gate[:, l * LANES:(l + 1) * LANES]).astype(y_ref.dtype)


def _rglru_seq(ug, cw, cb, wa_bd, wx_bd, ba, bx, lam, tc):
    nb, t, w2 = ug.shape
    w = w2 // 2
    assert nb == SUBLANES and t % tc == 0 and w % LANES == 0
    row = lambda a: a.reshape(1, w)
    return pl.pallas_call(
        _rglru_seq_kernel,
        grid=(t // tc,),
        in_specs=[pl.BlockSpec((nb, tc, w), lambda i: (0, i, 0)), pl.BlockSpec((nb, tc, w), lambda i: (0, i, 1)),
                  _const_spec(cw.shape), _const_spec((1, w)), _const_spec((w, w)), _const_spec((w, w)),
                  _const_spec((1, w)), _const_spec((1, w)), _const_spec((1, w))],
        out_specs=[pl.BlockSpec((nb, tc, w), lambda i: (0, i, 0)), pl.BlockSpec((nb, w), lambda i: (0, 0))],
        out_shape=[jax.ShapeDtypeStruct((nb, t, w), BF16), jax.ShapeDtypeStruct((nb, w), F32)],
        scratch_shapes=[pltpu.VMEM((w // LANES, tc * nb, LANES), F32), pltpu.VMEM((w // LANES, tc * nb, LANES), F32),
                        pltpu.VMEM((nb, SUBLANES, w), F32), pltpu.VMEM((w // LANES, nb, LANES), F32)],
        compiler_params=_cparams("arbitrary"), name="rglru_seq",
    )(ug, ug, cw, row(cb), wa_bd, wx_bd, row(ba), row(bx), row(lam))


def _rglru_step_kernel(u_ref, g_ref, cp_ref, h_ref, cw_ref, cb_ref, wa_ref, wx_ref, ba_ref, bx_ref, lam_ref,
                       y_ref, hn_ref):
    w = u_ref.shape[1]
    u = u_ref[...]
    uc = (cp_ref[:, 0:w] * cw_ref[0:1, :] + cp_ref[:, w:2 * w] * cw_ref[1:2, :]
          + cp_ref[:, 2 * w:3 * w] * cw_ref[2:3, :] + u * cw_ref[3:4, :] + cb_ref[...])
    a, bi = _rg_gate_math(uc, wa_ref, wx_ref, ba_ref, bx_ref, lam_ref)
    h = a * h_ref[...] + bi
    hn_ref[...] = h
    y_ref[...] = (h * _gelu(g_ref[...])).astype(y_ref.dtype)


def _rglru_step(ug, conv_prev, h_prev, cw, cb, wa_bd, wx_bd, ba, bx, lam):
    nb, w2 = ug.shape
    w = w2 // 2
    row = lambda a: a.reshape(1, w)
    full = lambda shape: pl.BlockSpec(shape, lambda i: (0,) * len(shape))
    return pl.pallas_call(
        _rglru_step_kernel,
        grid=(1,),
        in_specs=[pl.BlockSpec((nb, w), lambda i: (0, 0)), pl.BlockSpec((nb, w), lambda i: (0, 1)),
                  full((nb, 3 * w)), full((nb, w)), full(cw.shape), full((1, w)), full((w, w)), full((w, w)),
                  full((1, w)), full((1, w)), full((1, w))],
        out_specs=[full((nb, w)), full((nb, w))],
        out_shape=[jax.ShapeDtypeStruct((nb, w), BF16), jax.ShapeDtypeStruct((nb, w), F32)],
        compiler_params=_cparams("arbitrary"), name="rglru_step",
    )(ug, ug, conv_prev.reshape(nb, 3 * w), h_prev, cw, row(cb), wa_bd, wx_bd, row(ba), row(bx), row(lam))


ONES_ROWS = 16
LOG2E = 1.4426950408889634
FIXED_MAX_GAP = 64.0


def _flash_t(s, m_old, acc, vext):
    m_new = jnp.maximum(m_old, jnp.max(s, axis=0, keepdims=True))
    p = jnp.exp2(s - m_new).astype(BF16)
    return m_new, jnp.exp2(m_old - m_new) * acc + _dot(vext, p)


def _moba_seq_kernel(q_ref, k_ref, vt_ref, o_ref, km_s, qt_s, bias_s, acc_s, kmax_s, *, nblk):
    i = pl.program_id(2)
    blk = MOBA_BLOCK
    hd = HEAD_DIM

    @pl.when(i == 0)
    def _():
        for n in range(nblk):
            km_s[n:n + 1, :] = jnp.mean(k_ref[0, n * blk:(n + 1) * blk, :].astype(F32), axis=0, keepdims=True)
        kf = k_ref[0].astype(F32)
        head_of_col = lax.broadcasted_iota(I32, (kf.shape[1], LANES), 0) // hd
        onehot = (head_of_col == lax.broadcasted_iota(I32, (kf.shape[1], LANES), 1)).astype(F32)
        kmax_s[...] = jnp.max(jnp.dot(kf * kf, onehot, precision=_HI, preferred_element_type=F32), axis=0, keepdims=True)

    tq = q_ref.shape[1]
    nqb = tq // blk
    hps = q_ref.shape[2] // hd
    qt = q_ref[0].T
    row_head = lax.broadcasted_iota(I32, (LANES, 1), 0) // hd
    blk_id = lax.broadcasted_iota(I32, (nblk, 1), 0)
    own_blk = i * nqb + lax.broadcasted_iota(I32, (1, tq), 1) // blk
    causal = jnp.where(lax.broadcasted_iota(I32, (blk, 1), 0) <= lax.broadcasted_iota(I32, (1, blk), 1), 0.0, NEG)
    ones = jnp.ones((ONES_ROWS, blk), BF16)
    pair = lambda a, hh: a[:, (hh // 2) * LANES:(hh // 2 + 1) * LANES]

    score_cap = []
    for hh in range(hps):
        qth =jnp.where(row_head == hh % 2, qt[(hh // 2) * LANES:(hh // 2 + 1) * LANES], 0.0)
        gate = jnp.dot(pair(km_s[...], hh), qth, precision=_HI, preferred_element_type=F32)
        gate = jnp.where(blk_id < own_blk, gate, -jnp.inf)
        rank = jnp.zeros(gate.shape, I32)
        for m in range(nblk):
            gm = gate[m:m + 1, :]
            rank = rank + jnp.where(gm > gate, 1, jnp.where(gm == gate, (m < blk_id).astype(I32), 0))
        bias_s[hh] = jnp.where(jnp.where(blk_id < own_blk, rank, MOBA_TOPK) < MOBA_TOPK, 0.0, NEG)
        qs = qth * (hd ** -0.5 * LOG2E)
        qt_s[hh] = qs.astype(BF16)
        acc_s[hh] = jnp.zeros(acc_s.shape[1:], F32)
        score_cap.append(jnp.sqrt(jnp.sum(qs * qs, axis=0, keepdims=True)) * (jnp.sqrt(kmax_s[:, hh:hh + 1]) * 1.01))

    def key_block(n, ms, mask_of, running_max):
        n0 = pl.multiple_of(n * blk, blk)
        kt = k_ref[0, pl.ds(n0, blk), :]
        scores = lambda hh: _dot(pair(kt, hh), qt_s[hh]) + mask_of(hh)
        s_next = scores(0)
        out = []
        for hh in range(hps):
            s = s_next
            if hh + 1 < hps:
                s_next = scores(hh + 1)
            vext = jnp.concatenate([vt_ref[0, hh * hd:(hh + 1) * hd, pl.ds(n0, blk)], ones], axis=0)
            if running_max:
                m_new, acc_s[hh] = _flash_t(s, ms[hh], acc_s[hh], vext)
            else:
                m_new = ms[hh]
                acc_s[hh] = acc_s[hh] + _dot(vext, jnp.exp2(s - m_new).astype(BF16))
            out.append(m_new)
        return tuple(out)

    ms = tuple(jnp.full((1, tq), NEG, F32) for _ in range(hps))
    for c in range(nqb):
        n = i * nqb + c

        def mask_of(hh, c=c, n=n):
            row = bias_s[hh, pl.ds(n, 1), :]
            parts = [jnp.full((blk, blk), NEG, F32) if a < c else causal if a == c else
                     jnp.broadcast_to(row[:, a * blk:(a + 1) * blk], (blk, blk)) for a in range(nqb)]
            return jnp.concatenate(parts, axis=1)

        ms = key_block(n, ms, mask_of, True)
    past = lambda running_max: lax.fori_loop(
        0, i * nqb, lambda n, ms: key_block(n, ms, lambda hh: bias_s[hh, pl.ds(n, 1), :], running_max), ms)
    gap = jnp.max(jnp.concatenate([score_cap[hh] - ms[hh] for hh in range(hps)], axis=0))
    lax.cond(gap < FIXED_MAX_GAP, lambda: past(False), lambda: past(True))
    ot = jnp.concatenate([acc_s[hh, 0:hd] / acc_s[hh, hd:hd + 1] for hh in range(hps)], axis=0)
    o_ref[0] = ot.T.astype(o_ref.dtype)


def _moba_seq(q, kb, vtb):
    nb, t, hw = q.shape
    blk = MOBA_BLOCK
    hps, nqb = MOBA_HEADS_PER_STEP, MOBA_QBLOCKS_PER_STEP
    gw, tq = hps * HEAD_DIM, nqb * blk
    assert t % tq == 0 and hw % gw == 0 and 2 * HEAD_DIM == LANES and hps % 2 == 0
    nblk = t // blk
    return pl.pallas_call(
        functools.partial(_moba_seq_kernel, nblk=nblk),
        grid=(nb, hw // gw, t // tq),
        in_specs=[pl.BlockSpec((1, tq, gw), lambda b, h, i: (b, i, h)),
                  pl.BlockSpec((1, t, gw), lambda b, h, i: (b, 0, h)),
                  pl.BlockSpec((1, gw, t), lambda b, h, i: (b, h, 0))],
        out_specs=pl.BlockSpec((1, tq, gw), lambda b, h, i: (b, i, h)),
        out_shape=jax.ShapeDtypeStruct((nb, t, hw), BF16),
        scratch_shapes=[pltpu.VMEM((nblk, gw), F32), pltpu.VMEM((hps, LANES, tq), BF16),
                        pltpu.VMEM((hps, nblk, tq), F32), pltpu.VMEM((hps, HEAD_DIM + ONES_ROWS, tq), F32),
                        pltpu.VMEM((1, LANES), F32)],
        compiler_params=_cparams("parallel", "parallel", "arbitrary"), name="moba_seq",
    )(q, kb, vtb)


def _dsa_seq_kernel(qi_ref, kiw_ref, ki_ref, q_ref, k_ref, vt_ref, o_ref, key_s, qit_s, qt_s, acc_s, kmax_s,
                    *, tk, nsel, idx_bits):
    i = pl.program_id(1)
    tq = qi_ref.shape[1]
    nkv, hd = DSA_KV_HEADS, HEAD_DIM
    grp = q_ref.shape[2] // hd // nkv
    nck = ((i + 1) * tq + tk - 1) // tk
    qpos = i * tq + lax.broadcasted_iota(I32, (1, tq), 1)
    krow = lax.broadcasted_iota(I32, (tk, 1), 0)

    qit = (qi_ref[0] * (IDX_DIM ** -0.5)).T.astype(BF16)
    qit_s[...] = jnp.concatenate([qit[h * IDX_DIM:(h + 1) * IDX_DIM] for h in range(IDX_HEADS)], axis=1)
    wt = kiw_ref[0].T[IDX_DIM:IDX_DIM + IDX_HEADS] * (IDX_HEADS ** -0.5)
    qtf = (q_ref[0] * (hd ** -0.5 * LOG2E)).T
    qt = qtf.astype(BF16)
    zero = jnp.zeros((hd, grp * tq), BF16)
    heads_of = lambda a, n: [a[(n * grp + g) * hd:(n * grp + g + 1) * hd] for g in range(grp)]
    for n in range(nkv):
        own = jnp.concatenate(heads_of(qt, n), axis=1)
        qt_s[n] = jnp.concatenate([own, zero] if n % 2 == 0 else [zero, own], axis=0)

    @pl.when(i == 0)
    def _():
        kf = k_ref[0].astype(F32)
        head_of_col = lax.broadcasted_iota(I32, (kf.shape[1], LANES), 0) // hd
        onehot = (head_of_col == lax.broadcasted_iota(I32, (kf.shape[1], LANES), 1)).astype(F32)
        norms2 = jnp.dot(kf * kf, onehot, precision=_HI, preferred_element_type=F32)
        kmax_s[...] = jnp.max(norms2, axis=0, keepdims=True)

    score_cap = [jnp.concatenate([jnp.sqrt(jnp.sum(h * h, axis=0, keepdims=True)) for h in heads_of(qtf, n)], axis=1)
                 * (jnp.sqrt(kmax_s[:, n:n + 1]) * 1.01) for n in range(nkv)]

    def score_chunk(c, carry):
        c0 = pl.multiple_of(c * tk, tk)
        d = _dot(ki_ref[0, pl.ds(c0, tk), :], qit_s[...])
        sc = jnp.zeros((tk, tq), F32)
        for h in range(IDX_HEADS):
            sc = sc + wt[h:h + 1, :] * jnp.maximum(d[:, h * tq:(h + 1) * tq], 0.0)
        sc = jnp.where(c0 + krow <= qpos, sc, -jnp.inf)
        key_s[pl.ds(c0, tk), :] = _sortable(sc)
        return carry

    lax.fori_loop(0, nck, score_chunk, 0)

    def count(pred):
        def body(c, acc):
            c0 = pl.multiple_of(c * tk, tk)
            hit = pred(key_s[pl.ds(c0, tk), :], c0)
            return acc + jnp.sum(hit.reshape(tk // SUBLANES, SUBLANES, tq), axis=0)
        acc = lax.fori_loop(0, nck, body, jnp.zeros((SUBLANES, tq), I32))
        return jnp.sum(acc, axis=0, keepdims=True)

    def value_bit(state):
        it, thr, above = state
        cand = thr + jnp.left_shift(jnp.int32(1), 31 - it)
        cnt = count(lambda key, c0: jnp.where(key >= cand, 1, 0))
        ok = cnt >= nsel
        return it + 1, jnp.where(ok, cand, thr), jnp.where(ok, cnt, above)

    _, thr, _ = lax.while_loop(lambda st: jnp.logical_and(st[0] < 32, jnp.max(st[2]) > nsel), value_bit,
                               (jnp.int32(0), jnp.full((1, tq), INT_MIN, I32), jnp.full((1, tq), nck * tk, I32)))
    need = nsel - count(lambda key, c0: jnp.where(key > thr, 1, 0))
    ties = count(lambda key, c0: jnp.where(key == thr, 1, 0))

    def index_bit(it, cut):
        cand = cut + jnp.left_shift(jnp.int32(1), idx_bits - 1 - it)
        cnt = count(lambda key, c0: jnp.where(key == thr, jnp.where(c0 + krow < cand, 1, 0), 0))
        return jnp.where(cnt < need, cand, cut)

    cut = lax.cond(jnp.max(ties - need) > 0,
                   lambda: lax.fori_loop(0, idx_bits, index_bit, jnp.zeros((1, tq), I32)),
                   lambda: jnp.full((1, tq), 2 ** idx_bits, I32))

    acc_s[...] = jnp.zeros(acc_s.shape, F32)
    ones = jnp.ones((ONES_ROWS, tk), BF16)

    def attend(c, ms, running_max):
        c0 = pl.multiple_of(c * tk, tk)
        key = key_s[pl.ds(c0, tk), :]
        kpos = c0 + krow
        take = jnp.where(key > thr, 1, jnp.where(key == thr, jnp.where(kpos <= cut, 1, 0), 0))
        take = jnp.where(kpos <= qpos, take, 0)
        bias = jnp.where(take > 0, 0.0, NEG)
        bias = jnp.concatenate([bias] * grp, axis=1)
        kc = k_ref[0, pl.ds(c0, tk), :]
        scores = lambda n: _dot(kc[:, (n // 2) * LANES:(n // 2 + 1) * LANES], qt_s[n]) + bias
        s_next = scores(0)
        out = []
        for n in range(nkv):
            s = s_next
            if n + 1 < nkv:
                s_next = scores(n + 1)
            vext = jnp.concatenate([vt_ref[0, n * hd:(n + 1) * hd, pl.ds(c0, tk)], ones], axis=0)
            if running_max:
                m_new, acc_s[n] = _flash_t(s, ms[n], acc_s[n], vext)
            else:
                m_new = ms[n]
                acc_s[n] = acc_s[n] + _dot(vext, jnp.exp2(s - m_new).astype(BF16))
            out.append(m_new)
        return tuple(out)

    ms = attend(0, tuple(jnp.full((1, grp * tq), NEG, F32) for _ in range(nkv)), True)
    gap = jnp.max(jnp.concatenate([score_cap[n] - ms[n] for n in range(nkv)], axis=0))
    lax.cond(gap < FIXED_MAX_GAP,
             lambda: lax.fori_loop(1, nck, lambda c, ms: attend(c, ms, False), ms),
             lambda: lax.fori_loop(1, nck, lambda c, ms: attend(c, ms, True), ms))
    pieces = []
    for n in range(nkv):
        on = acc_s[n, 0:hd] / acc_s[n, hd:hd + 1]
        pieces += [on[:, g * tq:(g + 1) * tq] for g in range(grp)]
    o_ref[0] = jnp.concatenate(pieces, axis=0).T.astype(o_ref.dtype)


def _dsa_seq(qi, kiw, kib, q, kb, vtb, tq, tk):
    nb, t, qw = q.shape
    assert t % tq == 0 and t % tk == 0 and tq == LANES and 2 * HEAD_DIM == LANES and DSA_KV_HEADS % 2 == 0
    nsel = min(DSA_TOPK, t // 4)
    kvw = kb.shape[2]
    grp = qw // HEAD_DIM // DSA_KV_HEADS
    return pl.pallas_call(
        functools.partial(_dsa_seq_kernel, tk=tk, nsel=nsel, idx_bits=max(1, (t - 1).bit_length())),
        grid=(nb, t // tq),
        in_specs=[pl.BlockSpec((1, tq, qi.shape[2]), lambda b, i: (b, i, 0)),
                  pl.BlockSpec((1, tq, LANES), lambda b, i: (b, i, 0)),
                  pl.BlockSpec((1, t, IDX_DIM), lambda b, i: (b, 0, 0)),
                  pl.BlockSpec((1, tq, qw), lambda b, i: (b, i, 0)),
                  pl.BlockSpec((1, t, kvw), lambda b, i: (b, 0, 0)),
                  pl.BlockSpec((1, kvw, t), lambda b, i: (b, 0, 0))],
        out_specs=pl.BlockSpec((1, tq, qw), lambda b, i: (b, i, 0)),
        out_shape=jax.ShapeDtypeStruct((nb, t, qw), BF16),
        scratch_shapes=[pltpu.VMEM((t, tq), I32), pltpu.VMEM((IDX_DIM, IDX_HEADS * tq), BF16),
                        pltpu.VMEM((DSA_KV_HEADS, LANES, grp * tq), BF16),
                        pltpu.VMEM((DSA_KV_HEADS, HEAD_DIM + ONES_ROWS, grp * tq), F32), pltpu.VMEM((1, LANES), F32)],
        compiler_params=_cparams("parallel", "arbitrary"), name="dsa_seq",
    )(qi, kiw, kib, q, kb, vtb)


def _ffn_core(x1, gn_ref, wg_ref, wu_ref, cw_ref, cb_ref, wd_ref, conv_fn, fc):
    xn = _rms(x1, gn_ref[...]).astype(BF16)
    dff = wg_ref.shape[1]
    acc = jnp.zeros(x1.shape, F32)
    for c in range(dff // fc):
        cs = slice(c * fc, (c + 1) * fc)
        g = _dot(xn, wg_ref[:, cs])
        u = _dot(xn, wu_ref[:, cs])
        gm2, gm1 = conv_fn(g, cs)
        gc = gm2 * cw_ref[0:1, cs] + gm1 * cw_ref[1:2, cs] + g * cw_ref[2:3, cs] + cb_ref[:, cs]
        acc = acc + _dot((_gelu(gc) * u).astype(BF16), wd_ref[cs, :])
    return x1 + acc


def _ffn_seq_kernel(*refs, n_y, tiles_per_seq, final, fc):
    x_ref = refs[0]
    y_refs = refs[1:1 + n_y]
    wo_refs = refs[1 + n_y:1 + 2 * n_y]
    gn_ref, wg_ref, wu_ref, cw_ref, cb_ref, wd_ref, gf_ref = refs[1 + 2 * n_y:8 + 2 * n_y]
    out_ref, st_ref, tail_s = refs[8 + 2 * n_y:]
    tm = x_ref.shape[0]

    @pl.when(pl.program_id(0) % tiles_per_seq == 0)
    def _():
        tail_s[...] = jnp.zeros_like(tail_s)

    x1 = x_ref[...]
    for y_ref, wo_ref in zip(y_refs, wo_refs):
        x1 = x1 + _dot(y_ref[...], wo_ref[...])

    def conv_fn(g, cs):
        prev = tail_s[:, cs]
        tail_s[:, cs] = g[tm - SUBLANES:tm]
        return _shift_rows(g, prev, 2), _shift_rows(g, prev, 1)

    x2 = _ffn_core(x1, gn_ref, wg_ref, wu_ref, cw_ref, cb_ref, wd_ref, conv_fn, fc)
    st_ref[0] = tail_s[...]
    out_ref[...] = _rms(x2, gf_ref[...]) if final else x2


def _ffn_step_kernel(*refs, n_y, final, fc):
    x_ref = refs[0]
    y_refs = refs[1:1 + n_y]
    wo_refs = refs[1 + n_y:1 + 2 * n_y]
    gn_ref, wg_ref, wu_ref, cw_ref, cb_ref, wd_ref, gf_ref, p_ref = refs[1 + 2 * n_y:9 + 2 * n_y]
    out_ref, g_ref = refs[9 + 2 * n_y:]
    dff = wg_ref.shape[1]

    x1 = x_ref[...]
    for y_ref, wo_ref in zip(y_refs, wo_refs):
        x1 = x1 + _dot(y_ref[...], wo_ref[...])

    def conv_fn(g, cs):
        g_ref[:, cs] = g
        return p_ref[:, cs], p_ref[:, dff + cs.start:dff + cs.stop]

    x2 = _ffn_core(x1, gn_ref, wg_ref, wu_ref, cw_ref, cb_ref, wd_ref, conv_fn, fc)
    out_ref[...] = _rms(x2, gf_ref[...]) if final else x2


def _ffn_weights_specs(d, dff):
    return [_const_spec((1, d)), _const_spec((d, dff)), _const_spec((d, dff)), _const_spec((3, dff)),
            _const_spec((1, dff)), _const_spec((dff, d)), _const_spec((1, d))]


def _ffn_seq(x2d, ys, wos, gn, wg, wu, cw, cb, wd, gfinal, final, seq_len, tm, fc=1024):
    n, d = x2d.shape
    dff = wg.shape[1]
    assert seq_len % tm == 0 and dff % fc == 0
    tps = seq_len // tm
    row_spec = lambda c: pl.BlockSpec((tm, c), lambda i: (i, 0))
    return pl.pallas_call(
        functools.partial(_ffn_seq_kernel, n_y=len(ys), tiles_per_seq=tps, final=final, fc=fc),
        grid=(n // tm,),
        in_specs=[row_spec(d)] + [row_spec(y.shape[1]) for y in ys] + [_const_spec(w.shape) for w in wos]
                 + _ffn_weights_specs(d, dff),
        out_specs=[row_spec(d), pl.BlockSpec((1, SUBLANES, dff), lambda i: (i // tps, 0, 0))],
        out_shape=[jax.ShapeDtypeStruct((n, d), F32), jax.ShapeDtypeStruct((n // seq_len, SUBLANES, dff), F32)],
        scratch_shapes=[pltpu.VMEM((SUBLANES, dff), F32)],
        compiler_params=_cparams("arbitrary"), name="ffn_seq",
    )(x2d, *ys, *wos, gn.reshape(1, d), wg, wu, cw, cb.reshape(1, dff), wd, gfinal.reshape(1, d))


def _ffn_step(x2d, ys, wos, gn, wg, wu, cw, cb, wd, gfinal, final, prev, fc=1024):
    n, d = x2d.shape
    dff = wg.shape[1]
    full = lambda a: pl.BlockSpec(a.shape, lambda i: (0,) * a.ndim)
    prev2 = prev.reshape(n, 2 * dff)
    return pl.pallas_call(
        functools.partial(_ffn_step_kernel, n_y=len(ys), final=final, fc=fc),
        grid=(1,),
        in_specs=[full(x2d)] + [full(y) for y in ys] + [_const_spec(w.shape) for w in wos]
                 + _ffn_weights_specs(d, dff) + [full(prev2)],
        out_specs=[pl.BlockSpec((n, d), lambda i: (0, 0)), pl.BlockSpec((n, dff), lambda i: (0, 0))],
        out_shape=[jax.ShapeDtypeStruct((n, d), F32), jax.ShapeDtypeStruct((n, dff), F32)],
        compiler_params=_cparams("arbitrary"), name="ffn_step",
    )(x2d, *ys, *wos, gn.reshape(1, d), wg, wu, cw, cb.reshape(1, dff), wd, gfinal.reshape(1, d), prev2)


def _moba_pick_kernel(pt_ref, q_ref, *refs, pages_per_step):
    page_refs = refs[:pages_per_step]
    sel_ref, tv_s, ti_s = refs[pages_per_step:]
    s = pl.program_id(1)
    ppb = MOBA_BLOCK // PAGE_SIZE
    bps = pages_per_step // ppb
    heads = q_ref.shape[1]

    @pl.when(s == 0)
    def _():
        tv_s[...] = jnp.full(tv_s.shape, -jnp.inf, F32)
        ti_s[...] = jnp.zeros(ti_s.shape, I32)

    q = q_ref[0]
    t1, t2, t3 = tv_s[0], tv_s[1], tv_s[2]
    i1, i2, i3 = ti_s[0], ti_s[1], ti_s[2]
    for j in range(bps):
        tot = jnp.zeros((heads, PAGE_SIZE), F32)
        for r in range(ppb):
            tot = tot + jnp.sum(page_refs[j * ppb + r][0] * q, axis=1)
        g = jnp.sum(tot, axis=1, keepdims=True) * (1.0 / MOBA_BLOCK)
        n = s * bps + j
        c1, c2, c3 = g > t1, g > t2, g > t3
        t3, i3 = jnp.where(c2, t2, jnp.where(c3, g, t3)), jnp.where(c2, i2, jnp.where(c3, n, i3))
        t2, i2 = jnp.where(c1, t1, jnp.where(c2, g, t2)), jnp.where(c1, i1, jnp.where(c2, n, i2))
        t1, i1 = jnp.where(c1, g, t1), jnp.where(c1, n, i1)
    tv_s[0], tv_s[1], tv_s[2] = t1, t2, t3
    ti_s[0], ti_s[1], ti_s[2] = i1, i2, i3

    @pl.when(s == pl.num_programs(1) - 1)
    def _():
        lane = lax.broadcasted_iota(I32, (heads, LANES), 1)
        sel_ref[0] = jnp.where(lane == 0, i1, jnp.where(lane == 1, i2, i3))


def _moba_pick(page_table, q, cache_kt, pages_per_step=PAGES_PER_STEP):
    nb, npg = page_table.shape
    _, heads, hd, _ = cache_kt.shape
    ppb = MOBA_BLOCK // PAGE_SIZE
    assert MOBA_TOPK == 3 and npg % pages_per_step == 0 and pages_per_step % ppb == 0 and npg // ppb >= MOBA_TOPK
    page_spec = lambda j: pl.BlockSpec((1, heads, hd, PAGE_SIZE),
                                       lambda b, s, pt: (pt[b, s * pages_per_step + j], 0, 0, 0))
    grid_spec = pltpu.PrefetchScalarGridSpec(
        num_scalar_prefetch=1, grid=(nb, npg // pages_per_step),
        in_specs=[pl.BlockSpec((1, heads, hd, 1), lambda b, s, pt: (b, 0, 0, 0))]
                 + [page_spec(j) for j in range(pages_per_step)],
        out_specs=pl.BlockSpec((1, heads, LANES), lambda b, s, pt: (b, 0, 0)),
        scratch_shapes=[pltpu.VMEM((MOBA_TOPK, heads, 1), F32), pltpu.VMEM((MOBA_TOPK, heads, 1), I32)])
    return pl.pallas_call(
        functools.partial(_moba_pick_kernel, pages_per_step=pages_per_step),
        grid_spec=grid_spec, out_shape=jax.ShapeDtypeStruct((nb, heads, LANES), I32),
        compiler_params=_cparams("parallel", "arbitrary"), name="moba_pick",
    )(page_table, q.reshape(nb, heads, hd, 1), *([cache_kt] * pages_per_step))


def _moba_step_kernel(sel_ref, pt_ref, q_ref, kn_ref, vn_ref, *refs, npage):
    k_refs = refs[:npage]
    v_refs = refs[npage:2 * npage]
    o_ref = refs[2 * npage]
    hd = q_ref.shape[3]
    q8 = jnp.broadcast_to(q_ref[0, 0] * (HEAD_DIM ** -0.5), (SUBLANES, hd))
    s_new = jnp.sum(q8 * kn_ref[0, 0], axis=1, keepdims=True)
    qb = q8.astype(BF16)
    s = jnp.concatenate([_dot(qb, k_refs[j][0, 0].astype(BF16)) for j in range(npage)], axis=1)
    m = jnp.maximum(s_new, jnp.max(s, axis=1, keepdims=True))
    p_new = jnp.exp(s_new - m)
    p = jnp.exp(s - m)
    l = p_new + jnp.sum(p, axis=1, keepdims=True)
    pb = p.astype(BF16)
    acc = p_new * vn_ref[0, 0]
    for j in range(npage):
        acc = acc + _dot_t(pb[:, j * PAGE_SIZE:(j + 1) * PAGE_SIZE], v_refs[j][0, 0].astype(BF16))
    o_ref[0, 0] = (acc / l)[0:1]


def _moba_step(sel, page_table, q, k_new, v_new, cache_kt, cache_vt):
    nb, heads, hd = q.shape
    ppb = MOBA_BLOCK // PAGE_SIZE
    npage = MOBA_TOPK * ppb

    def page_spec(j):
        r, pg = divmod(j, ppb)
        return pl.BlockSpec((1, 1, hd, PAGE_SIZE),
                            lambda b, h, sel, pt: (pt[b, sel[b, h * MOBA_TOPK + r] * ppb + pg], h, 0, 0))

    vec = pl.BlockSpec((1, 1, 1, hd), lambda b, h, sel, pt: (b, h, 0, 0))
    grid_spec = pltpu.PrefetchScalarGridSpec(
        num_scalar_prefetch=2, grid=(nb, heads),
        in_specs=[vec, vec, vec] + [page_spec(j) for j in range(npage)] * 2,
        out_specs=vec)
    r4 = lambda a: a.reshape(nb, heads, 1, hd)
    return pl.pallas_call(
        functools.partial(_moba_step_kernel, npage=npage),
        grid_spec=grid_spec, out_shape=jax.ShapeDtypeStruct((nb, heads, 1, hd), F32),
        compiler_params=_cparams("parallel", "arbitrary"), name="moba_step",
    )(sel, page_table, r4(q), r4(k_new), r4(v_new), *([cache_kt] * npage), *([cache_vt] * npage)).reshape(nb, heads * hd)


def _dsa_score_kernel(pt_ref, qi_ref, w_ref, *refs, pages_per_step):
    page_refs = refs[:pages_per_step]
    sc_ref = refs[pages_per_step]
    qi = qi_ref[0] * (IDX_DIM ** -0.5)
    w = w_ref[0] * (IDX_HEADS ** -0.5)
    for j in range(pages_per_step):
        d = _dot(qi.astype(BF16), page_refs[j][0].astype(BF16))
        sc_ref[0, j:j + 1, :] = jnp.sum(w * jnp.maximum(d, 0.0), axis=0, keepdims=True)


def _dsa_score(page_table, qi, wi, cache_ki, pages_per_step=PAGES_PER_STEP):
    nb, npg = page_table.shape
    assert npg % pages_per_step == 0
    page_spec = lambda j: pl.BlockSpec((1, IDX_DIM, PAGE_SIZE), lambda b, s, pt: (pt[b, s * pages_per_step + j], 0, 0))
    grid_spec = pltpu.PrefetchScalarGridSpec(
        num_scalar_prefetch=1, grid=(nb, npg // pages_per_step),
        in_specs=[pl.BlockSpec((1, IDX_HEADS, IDX_DIM), lambda b, s, pt: (b, 0, 0)),
                  pl.BlockSpec((1, IDX_HEADS, 1), lambda b, s, pt: (b, 0, 0))]
                 + [page_spec(j) for j in range(pages_per_step)],
        out_specs=pl.BlockSpec((1, pages_per_step, PAGE_SIZE), lambda b, s, pt: (b, s, 0)))
    return pl.pallas_call(
        functools.partial(_dsa_score_kernel, pages_per_step=pages_per_step),
        grid_spec=grid_spec, out_shape=jax.ShapeDtypeStruct((nb, npg, PAGE_SIZE), F32),
        compiler_params=_cparams("parallel", "arbitrary"), name="dsa_score",
    )(page_table, qi, wi, *([cache_ki] * pages_per_step))


def _dsa_pick_kernel(sc_ref, qi_ref, w_ref, kin_ref, bias_ref, bnew_ref, *, nsel, idx_bits):
    nb, npg, _ = sc_ref.shape
    d = jnp.sum(qi_ref[...] * (IDX_DIM ** -0.5) * kin_ref[...], axis=1, keepdims=True)
    per_head = (w_ref[...] * (IDX_HEADS ** -0.5) * jnp.maximum(d, 0.0)).reshape(nb, IDX_HEADS, 1)
    key_new = _sortable(jnp.sum(per_head, axis=1, keepdims=True))
    key = _sortable(sc_ref[...])
    pos = (lax.broadcasted_iota(I32, key.shape, 1) * PAGE_SIZE + lax.broadcasted_iota(I32, key.shape, 2))
    pos_new = npg * PAGE_SIZE

    def total(hit, hit_new):
        return jnp.sum(jnp.sum(hit, axis=1, keepdims=True), axis=2, keepdims=True) + hit_new

    def value_bit(it, thr):
        cand = thr + jnp.left_shift(jnp.int32(1), 31 - it)
        cnt = total(jnp.where(key >= cand, 1, 0), jnp.where(key_new >= cand, 1, 0))
        return jnp.where(cnt >= nsel, cand, thr)

    thr = lax.fori_loop(0, 32, value_bit, jnp.full((nb, 1, 1), INT_MIN, I32))
    need = nsel - total(jnp.where(key > thr, 1, 0), jnp.where(key_new > thr, 1, 0))

    def index_bit(it, cut):
        cand = cut + jnp.left_shift(jnp.int32(1), idx_bits - 1 - it)
        cnt = total(jnp.where(key == thr, jnp.where(pos < cand, 1, 0), 0),
                    jnp.where(key_new == thr, jnp.where(pos_new < cand, 1, 0), 0))
        return jnp.where(cnt < need, cand, cut)

    cut = lax.fori_loop(0, idx_bits, index_bit, jnp.zeros((nb, 1, 1), I32))
    take = jnp.where(key > thr, 1, jnp.where(key == thr, jnp.where(pos <= cut, 1, 0), 0))
    bias_ref[...] = jnp.where(take > 0, 0.0, NEG)
    take_new = jnp.where(key_new > thr, 1, jnp.where(key_new == thr, jnp.where(pos_new <= cut, 1, 0), 0))
    bnew_ref[...] = jnp.broadcast_to(jnp.where(take_new > 0, 0.0, NEG), bnew_ref.shape)


def _dsa_pick(scores, qi, wi, ki_new):
    nb, npg, _ = scores.shape
    total_len = npg * PAGE_SIZE + 1
    rows = nb * IDX_HEADS
    full = lambda shape: pl.BlockSpec(shape, lambda i: (0,) * len(shape))
    return pl.pallas_call(
        functools.partial(_dsa_pick_kernel, nsel=min(DSA_TOPK, total_len // 4),
                          idx_bits=max(1, (total_len - 1).bit_length())),
        grid=(1,),
        in_specs=[full(scores.shape), full((rows, IDX_DIM)), full((rows, 1)), full((rows, IDX_DIM))],
        out_specs=[full(scores.shape), full((nb, 1, LANES))],
        out_shape=[jax.ShapeDtypeStruct(scores.shape, F32), jax.ShapeDtypeStruct((nb, 1, LANES), F32)],
        compiler_params=_cparams("arbitrary"), name="dsa_pick",
    )(scores, qi.reshape(rows, IDX_DIM), wi.reshape(rows, 1), jnp.repeat(ki_new, IDX_HEADS, axis=0))


def _dsa_step_kernel(pt_ref, bias_ref, bnew_ref, q_ref, kn_ref, vn_ref, *refs, pages_per_step):
    k_refs = refs[:pages_per_step]
    v_refs = refs[pages_per_step:2 * pages_per_step]
    o_ref, m_s, l_s, acc_s = refs[2 * pages_per_step:]
    s = pl.program_id(1)
    nh = q_ref.shape[1]
    kvw = kn_ref.shape[2]
    grp = nh // DSA_KV_HEADS
    row_kv = lax.broadcasted_iota(I32, (nh, kvw), 0) // grp
    lane_kv = lax.broadcasted_iota(I32, (nh, kvw), 1) // HEAD_DIM
    own = row_kv == lane_kv
    q = q_ref[0] * (HEAD_DIM ** -0.5)
    qbd = jnp.where(own, jnp.concatenate([q] * DSA_KV_HEADS, axis=1), 0.0)

    @pl.when(s == 0)
    def _():
        m_s[...] = jnp.sum(qbd * kn_ref[0], axis=1, keepdims=True) + bnew_ref[0][:, 0:1]
        l_s[...] = jnp.ones_like(l_s)
        acc_s[...] = jnp.broadcast_to(vn_ref[0], acc_s.shape)

    m, l, acc = m_s[...], l_s[...], acc_s[...]
    qb = qbd.astype(BF16)
    sc = jnp.concatenate(
        [_dot(qb, k_refs[j][0].reshape(kvw, PAGE_SIZE).astype(BF16)) + bias_ref[0, pl.ds(s * pages_per_step + j, 1), :]
         for j in range(pages_per_step)], axis=1)
    m_new = jnp.maximum(m, jnp.max(sc, axis=1, keepdims=True))
    alpha = jnp.exp(m - m_new)
    p = jnp.exp(sc - m_new)
    l = alpha * l + jnp.sum(p, axis=1, keepdims=True)
    pb = p.astype(BF16)
    acc = alpha * acc
    for j in range(pages_per_step):
        vt = v_refs[j][0].reshape(kvw, PAGE_SIZE).astype(BF16)
        acc = acc + _dot_t(pb[:, j * PAGE_SIZE:(j + 1) * PAGE_SIZE], vt)
    m = m_new
    m_s[...], l_s[...], acc_s[...] = m, l, acc

    @pl.when(s == pl.num_programs(1) - 1)
    def _():
        on = jnp.where(own, acc / l, 0.0)
        o = on[:, 0:HEAD_DIM]
        for n in range(1, DSA_KV_HEADS):
            o = o + on[:, n * HEAD_DIM:(n + 1) * HEAD_DIM]
        o_ref[0] = o


def _dsa_step(page_table, bias, bias_new, q, k_new, v_new, cache_k, cache_v, pages_per_step=PAGES_PER_STEP):
    nb, npg = page_table.shape
    nh = q.shape[1]
    kvw = k_new.shape[1]
    bmap = lambda b, s, pt: (b, 0, 0)
    page_spec = lambda j: pl.BlockSpec((1, DSA_KV_HEADS, HEAD_DIM, PAGE_SIZE),
                                       lambda b, s, pt: (pt[b, s * pages_per_step + j], 0, 0, 0))
    grid_spec = pltpu.PrefetchScalarGridSpec(
        num_scalar_prefetch=1, grid=(nb, npg // pages_per_step),
        in_specs=[pl.BlockSpec((1, npg, PAGE_SIZE), bmap), pl.BlockSpec((1, 1, LANES), bmap),
                  pl.BlockSpec((1, nh, HEAD_DIM), bmap), pl.BlockSpec((1, 1, kvw), bmap), pl.BlockSpec((1, 1, kvw), bmap)]
                 + [page_spec(j) for j in range(pages_per_step)] * 2,
        out_specs=pl.BlockSpec((1, nh, HEAD_DIM), bmap),
        scratch_shapes=[pltpu.VMEM((nh, 1), F32), pltpu.VMEM((nh, 1), F32), pltpu.VMEM((nh, kvw), F32)])
    return pl.pallas_call(
        functools.partial(_dsa_step_kernel, pages_per_step=pages_per_step),
        grid_spec=grid_spec, out_shape=jax.ShapeDtypeStruct((nb, nh, HEAD_DIM), F32),
        compiler_params=_cparams("parallel", "arbitrary"), name="dsa_step",
    )(page_table, bias, bias_new, q, k_new.reshape(nb, 1, kvw), v_new.reshape(nb, 1, kvw),
      *([cache_k] * pages_per_step), *([cache_v] * pages_per_step))


def _in0_segs(w, step):
    rows = ((F32, "rows"),)
    if step:
        return ((0, 2 * w, None, 1.0, rows), (2 * w, w, "heads", 1.0, rows), (3 * w, w, "heads", 1.0, rows),
                (4 * w, w, None, 1.0, rows))
    return ((0, 2 * w, None, 1.0, rows),
            (2 * w, w, "heads", 1.0, rows),
            (3 * w, w, "heads", 1.0, ((F32, "cols"), (BF16, "rows"))),
            (4 * w, w, None, 1.0, ((F32, "cols"), (BF16, "cols"))))


def _in1_segs(qw, kvw, iw, step):
    rows = ((F32, "rows"),)
    last = qw + 2 * kvw + iw
    if step:
        return ((0, qw, "heads", 1.0, rows), (qw, kvw, "heads", 1.0, rows), (qw + kvw, kvw, None, 1.0, rows),
                (qw + 2 * kvw, iw, "heads", 1.0, rows), (last, LANES, "half", 1.0, rows + ((F32, "half_rows"),)))
    return ((0, qw, "heads", 1.0, rows),
            (qw, kvw, "heads", 1.0, ((F32, "cols"), (BF16, "rows"))),
            (qw + kvw, kvw, None, 1.0, ((F32, "cols"), (BF16, "cols"))),
            (qw + 2 * kvw, iw, "heads", 1.0, rows),
            (last, LANES, "half", 1.0, rows + ((F32, "half_cols"), (BF16, "half_rows"))))


def _prep_weights(p):
    d = p["w_in0"].shape[0]
    w = p["rg_lambda"].shape[0]
    nblocks = p["rg_gate_a_w"].shape[0]
    eye = jnp.eye(nblocks, dtype=F32)
    bd = lambda g: jnp.einsum("ncd,nm->ncmd", g, eye).reshape(w, w).astype(BF16)
    w_in1 = p["w_in1"]
    pad = (-w_in1.shape[1]) % LANES
    return dict(
        w_in0=p["w_in0"].astype(BF16), wa=bd(p["rg_gate_a_w"]), wx=bd(p["rg_gate_x_w"]),
        w_out0a=p["w_out0"][:w].astype(BF16), w_out0b=p["w_out0"][w:].astype(BF16),
        w_in1=jnp.pad(w_in1, ((0, 0), (0, pad))).astype(BF16), w_out1=p["w_out1"].astype(BF16),
        wg=p["ffn_w_gate"].astype(BF16), wu=p["ffn_w_up"].astype(BF16), wd=p["ffn_w_down"].astype(BF16))


def _prompt_group(x, p, wb):
    nb, t, d = x.shape
    n = nb * t
    w = p["rg_lambda"].shape[0]
    tm = ROW_TILE
    tabs = _rope_tables(jnp.arange(t, dtype=I32), t)
    x2 = x.reshape(n, d)

    ug, q0, k0t, k0b, v0t, v0tb = _norm_proj(x2, p["norm_mix"][0], wb["w_in0"], tabs, _in0_segs(w, False), tm,
                                             t // tm)
    ug3 = ug.reshape(nb, t, 2 * w)
    y_rg, h_last = _rglru_seq(ug3, p["rg_conv_w"], p["rg_conv_b"], wb["wa"], wb["wx"], p["rg_gate_a_b"],
                              p["rg_gate_x_b"], p["rg_lambda"], tc=SCAN_TILE)
    o0 = _moba_seq(q0.reshape(nb, t, w), k0b.reshape(nb, t, w), v0tb)
    x3, st0 = _ffn_seq(x2, [y_rg.reshape(n, w), o0.reshape(n, w)], [wb["w_out0a"], wb["w_out0b"]],
                       p["norm_ffn"][0], wb["wg"][0], wb["wu"][0], p["ffn_conv_w"][0], p["ffn_conv_b"][0],
                       wb["wd"][0], p["norm_final"], False, t, tm)

    qw = p["w_out1"].shape[0]
    kvw = DSA_KV_HEADS * HEAD_DIM
    iw = IDX_HEADS * IDX_DIM
    q1, k1t, k1b, v1t, v1tb, qi, kiw, kit, kib = _norm_proj(
        x3, p["norm_mix"][1], wb["w_in1"], tabs, _in1_segs(qw, kvw, iw, False), tm, t // tm)
    r3 = lambda a: a.reshape(nb, t, a.shape[1])
    o1 = _dsa_seq(r3(qi), r3(kiw), r3(kib), r3(q1), r3(k1b), v1tb, tq=DSA_Q_TILE, tk=DSA_K_TILE)
    y, st1 = _ffn_seq(x3, [o1.reshape(n, qw)], [wb["w_out1"]], p["norm_ffn"][1], wb["wg"][1], wb["wu"][1],
                      p["ffn_conv_w"][1], p["ffn_conv_b"][1], wb["wd"][1], p["norm_final"], True, t, tm)

    heads = w // HEAD_DIM
    ffn_state = jnp.stack([st0[:, SUBLANES - 2:], st1[:, SUBLANES - 2:]])
    per_token = lambda a, h: a.reshape(nb, h, HEAD_DIM, t).transpose(0, 3, 1, 2)
    return (y.reshape(nb, t, d), per_token(k0t, heads), per_token(v0t, heads), h_last, ug3[:, t - 3:, :w],
            per_token(k1t, DSA_KV_HEADS), per_token(v1t, DSA_KV_HEADS), kit.transpose(0, 2, 1), ffn_state)


def _sample_group(x, p, wb, cache_k0, cache_v0, state_h0, state_conv0, cache_k1, cache_v1, cache_kidx1,
                  state_ffn, page_table):
    nb, t, d = x.shape
    assert t == 1
    w = p["rg_lambda"].shape[0]
    npg = page_table.shape[1]
    assert (npg * PAGE_SIZE) % MOBA_BLOCK == 0
    tabs = _rope_tables(jnp.full((1,), npg * PAGE_SIZE, I32), nb)
    x2 = x.reshape(nb, d)

    ug, q0, k0, v0 = _norm_proj(x2, p["norm_mix"][0], wb["w_in0"], tabs, _in0_segs(w, True), nb, 1)
    y_rg, h_new = _rglru_step(ug, state_conv0, state_h0, p["rg_conv_w"], p["rg_conv_b"], wb["wa"], wb["wx"],
                              p["rg_gate_a_b"], p["rg_gate_x_b"], p["rg_lambda"])
    heads = w // HEAD_DIM
    paged_t = lambda c: c.transpose(0, 2, 3, 1)
    ck0, cv0 = paged_t(cache_k0), paged_t(cache_v0)
    hsplit = lambda a: a.reshape(nb, heads, HEAD_DIM)
    picked = _moba_pick(page_table, hsplit(q0), ck0)
    sel = picked[:, :, :MOBA_TOPK].reshape(nb, heads * MOBA_TOPK)
    o0 = _moba_step(sel, page_table, hsplit(q0), hsplit(k0), hsplit(v0), ck0, cv0)
    x3, g0 = _ffn_step(x2, [y_rg, o0.astype(BF16)], [wb["w_out0a"], wb["w_out0b"]], p["norm_ffn"][0], wb["wg"][0],
                       wb["wu"][0], p["ffn_conv_w"][0], p["ffn_conv_b"][0], wb["wd"][0], p["norm_final"], False,
                       state_ffn[0])

    qw = p["w_out1"].shape[0]
    kvw = DSA_KV_HEADS * HEAD_DIM
    iw = IDX_HEADS * IDX_DIM
    q1, k1, v1, qi, kiw, ki = _norm_proj(
        x3, p["norm_mix"][1], wb["w_in1"], tabs, _in1_segs(qw, kvw, iw, True), nb, 1)
    qi3 = qi.reshape(nb, IDX_HEADS, IDX_DIM)
    wi3 = kiw[:, IDX_DIM:IDX_DIM + IDX_HEADS].reshape(nb, IDX_HEADS, 1)
    scores = _dsa_score(page_table, qi3, wi3, cache_kidx1.transpose(0, 2, 1))
    bias, bias_new = _dsa_pick(scores, qi3, wi3, ki)
    o1 = _dsa_step(page_table, bias, bias_new, q1.reshape(nb, qw // HEAD_DIM, HEAD_DIM), k1, v1,
                   paged_t(cache_k1), paged_t(cache_v1))
    y, g1 = _ffn_step(x3, [o1.reshape(nb, qw).astype(BF16)], [wb["w_out1"]], p["norm_ffn"][1], wb["wg"][1],
                      wb["wu"][1], p["ffn_conv_w"][1], p["ffn_conv_b"][1], wb["wd"][1], p["norm_final"], True,
                      state_ffn[1])

    conv_new = jnp.concatenate([state_conv0[:, 1:], ug[:, None, :w]], axis=1)
    ffn_state = jnp.stack([jnp.stack([state_ffn[0][:, 1], g0], axis=1), jnp.stack([state_ffn[1][:, 1], g1], axis=1)])
    return (y.reshape(nb, 1, d), k0.reshape(nb, 1, heads, HEAD_DIM), v0.reshape(nb, 1, heads, HEAD_DIM), h_new,
            conv_new, k1.reshape(nb, 1, DSA_KV_HEADS, HEAD_DIM), v1.reshape(nb, 1, DSA_KV_HEADS, HEAD_DIM),
            ki.reshape(nb, 1, IDX_DIM), ffn_state)


def kernel(x_prompt, x_sample, cache_k0, cache_v0, state_h0, state_conv0, cache_k1, cache_v1, cache_kidx1,
           state_ffn, page_table, norm_mix, norm_ffn, norm_final, w_in0, rg_conv_w, rg_conv_b, rg_gate_a_w,
           rg_gate_a_b, rg_gate_x_w, rg_gate_x_b, rg_lambda, w_out0, w_in1, w_out1, ffn_w_gate, ffn_w_up,
           ffn_conv_w, ffn_conv_b, ffn_w_down):
    p = dict(norm_mix=norm_mix, norm_ffn=norm_ffn, norm_final=norm_final, w_in0=w_in0, rg_conv_w=rg_conv_w,
             rg_conv_b=rg_conv_b, rg_gate_a_w=rg_gate_a_w, rg_gate_a_b=rg_gate_a_b, rg_gate_x_w=rg_gate_x_w,
             rg_gate_x_b=rg_gate_x_b, rg_lambda=rg_lambda, w_out0=w_out0, w_in1=w_in1, w_out1=w_out1,
             ffn_w_gate=ffn_w_gate, ffn_w_up=ffn_w_up, ffn_conv_w=ffn_conv_w, ffn_conv_b=ffn_conv_b,
             ffn_w_down=ffn_w_down)
    wb = _prep_weights(p)
    (y_p, k0_p, v0_p, h0_p, conv0_p, k1_p, v1_p, kidx1_p, ffn_p) = _prompt_group(x_prompt, p, wb)
    (y_s, k0_s, v0_s, h0_s, conv0_s, k1_s, v1_s, kidx1_s, ffn_s) = _sample_group(
        x_sample, p, wb, cache_k0, cache_v0, state_h0, state_conv0, cache_k1, cache_v1, cache_kidx1, state_ffn,
        page_table)
    return (y_p, y_s, k0_p, v0_p, h0_p, conv0_p, k1_p, v1_p, kidx1_p, ffn_p,
            k0_s, v0_s, h0_s, conv0_s, k1_s, v1_s, kidx1_s, ffn_s)
```

```python
import functools

import jax
import jax.numpy as jnp
from jax import lax
from jax.experimental import pallas as pl
from jax.experimental.pallas import tpu as pltpu

F32 = jnp.float32
BF16 = jnp.bfloat16
I32 = jnp.int32

HEAD_DIM = 64
PAGE_SIZE = 128
RG_C = 8.0
MOBA_BLOCK = 256
MOBA_TOPK = 3
DSA_KV_HEADS = 4
IDX_HEADS = 8
IDX_DIM = 64
DSA_TOPK = 256
ROPE_THETA = 10000.0
EPS = 1e-6

LANES = 128
SUBLANES = 8
VMEM_LIMIT = 56 * 1024 * 1024
NEG = -1e30
INT_MIN = -(2 ** 31)

_HI = lax.Precision.HIGHEST

ROW_TILE = 512
SCAN_TILE = 256
DSA_Q_TILE = LANES
DSA_K_TILE = 512
MOBA_HEADS_PER_STEP = 4
MOBA_QBLOCKS_PER_STEP = 2
PAGES_PER_STEP = 32


def _cparams(*sem):
    return pltpu.CompilerParams(dimension_semantics=sem, vmem_limit_bytes=VMEM_LIMIT)


def _const_spec(shape):
    nd = len(shape)
    return pl.BlockSpec(shape, lambda *_: (0,) * nd, pipeline_mode=pl.Buffered(1))


def _gelu(x):
    return x * (0.5 * (1.0 + jnp.tanh(0.7978845608028654 * (x + 0.044715 * (x * x * x)))))


def _sigmoid(x):
    return 1.0 / (1.0 + jnp.exp(-x))


def _rms(x, g):
    return x * lax.rsqrt(jnp.mean(x * x, axis=-1, keepdims=True) + EPS) * g


def _dot(a, b):
    return jnp.dot(a, b, preferred_element_type=F32)


def _dot_t(a, b, precision=None):
    return lax.dot_general(a, b, (((1,), (1,)), ((), ())), precision=precision,
                           preferred_element_type=F32)


def _rope_group(x, cos, sin_signed):
    lane = lax.broadcasted_iota(I32, x.shape, 1)
    first_half = (lane % HEAD_DIM) < (HEAD_DIM // 2)
    partner = jnp.where(first_half, pltpu.roll(x, LANES - HEAD_DIM // 2, 1), pltpu.roll(x, HEAD_DIM // 2, 1))
    return x * cos + partner * sin_signed


def _shift_rows(x, prev, j):
    r = pltpu.roll(x, j, 0)
    row = lax.broadcasted_iota(I32, (SUBLANES, x.shape[1]), 0)
    head = jnp.where(row < j, pltpu.roll(prev, j, 0), r[0:SUBLANES])
    if x.shape[0] == SUBLANES:
        return head
    return jnp.concatenate([head, r[SUBLANES:]], axis=0)


def _sortable(x):
    b = pltpu.bitcast(x, I32)
    return b ^ ((b >> 31) & 0x7FFFFFFF)


def _norm_proj_kernel(x_ref, g_ref, w_ref, cos_ref, sin_ref, cosk_ref, sink_ref, *out_refs, segs):
    xb = _rms(x_ref[...], g_ref[...]).astype(BF16)
    oi = 0
    for c0, width, rope, scale, outs in segs:
        for g0 in range(0, width, LANES):
            y = _dot(xb, w_ref[:, c0 + g0:c0 + g0 + LANES])
            if rope == "heads":
                y = _rope_group(y, cos_ref[...], sin_ref[...])
            elif rope == "half":
                y = _rope_group(y, cosk_ref[...], sink_ref[...])
            if scale != 1.0:
                y = y * scale
            yt = y.T if any(kind.endswith("cols") for _, kind in outs) else None
            for k, (dt, kind) in enumerate(outs):
                o_ref = out_refs[oi + k]
                if kind == "rows":
                    o_ref[:, g0:g0 + LANES] = y.astype(dt)
                elif kind == "cols":
                    o_ref[0, g0:g0 + LANES, :] = yt.astype(dt)
                elif kind == "half_rows":
                    o_ref[...] = y[:, :LANES // 2].astype(dt)
                else:
                    o_ref[0] = yt[:LANES // 2].astype(dt)
        oi += len(outs)


def _norm_proj(x2d, gamma, w_bf16, tables, segs, tm, n_tab):
    n, d = x2d.shape
    seq = n_tab * tm
    out_shape, out_specs = [], []
    for c0, width, rope, scale, outs in segs:
        for dt, kind in outs:
            wd = width if kind in ("rows", "cols") else LANES // 2
            if kind.endswith("rows"):
                out_shape.append(jax.ShapeDtypeStruct((n, wd), dt))
                out_specs.append(pl.BlockSpec((tm, wd), lambda i: (i, 0)))
            else:
                out_shape.append(jax.ShapeDtypeStruct((n // seq, wd, seq), dt))
                out_specs.append(pl.BlockSpec((1, wd, tm), lambda i: (i // n_tab, 0, i % n_tab)))
    tab_spec = pl.BlockSpec((tm, LANES), lambda i: (i % n_tab, 0))
    return pl.pallas_call(
        functools.partial(_norm_proj_kernel, segs=segs),
        grid=(n // tm,),
        in_specs=[pl.BlockSpec((tm, d), lambda i: (i, 0)), _const_spec((1, d)), _const_spec(w_bf16.shape),
                  tab_spec, tab_spec, tab_spec, tab_spec],
        out_specs=out_specs, out_shape=out_shape,
        compiler_params=_cparams("parallel"), name="norm_proj",
    )(x2d, gamma.reshape(1, d), w_bf16, *tables)


def _rope_tables(pos, rows):
    half = HEAD_DIM // 2
    inv = ROPE_THETA ** (-jnp.arange(half, dtype=F32) * 2.0 / HEAD_DIM)
    ang = pos.astype(F32)[:, None] * inv[None, :]
    cos, sin = jnp.cos(ang), jnp.sin(ang)
    one, zero = jnp.ones_like(cos), jnp.zeros_like(sin)
    tabs = [jnp.concatenate([cos, cos, cos, cos], -1), jnp.concatenate([-sin, sin, -sin, sin], -1),
            jnp.concatenate([cos, cos, one, one], -1), jnp.concatenate([-sin, sin, zero, zero], -1)]
    return [jnp.broadcast_to(t, (rows, LANES)) for t in tabs]


def _rg_gate_math(uc, wa_ref, wx_ref, ba_ref, bx_ref, lam_ref):
    ub = uc.astype(BF16)
    r = _sigmoid(_dot(ub, wa_ref[...]) + ba_ref[...])
    i = _sigmoid(_dot(ub, wx_ref[...]) + bx_ref[...])
    nl = -lam_ref[...]
    softplus = jnp.maximum(nl, 0.0) + jnp.log1p(jnp.exp(-jnp.abs(nl)))
    log_a = (-RG_C * softplus) * r
    a = jnp.exp(log_a)
    th = jnp.tanh(log_a)
    b = jnp.sqrt(-2.0 * th / (1.0 - th)) * i * uc
    return a, b


def _rglru_seq_kernel(u_ref, g_ref, cw_ref, cb_ref, wa_ref, wx_ref, ba_ref, bx_ref, lam_ref,
                      y_ref, hl_ref, a_s, b_s, tail_s, h_s):
    nb, tc, w = u_ref.shape
    nslab = w // LANES

    @pl.when(pl.program_id(0) == 0)
    def _():
        tail_s[...] = jnp.zeros_like(tail_s)
        h_s[...] = jnp.zeros_like(h_s)

    for b in range(nb):
        u = u_ref[b]
        prev = tail_s[b]
        uc = (_shift_rows(u, prev, 3) * cw_ref[0:1, :] + _shift_rows(u, prev, 2) * cw_ref[1:2, :]
              + _shift_rows(u, prev, 1) * cw_ref[2:3, :] + u * cw_ref[3:4, :] + cb_ref[...])
        tail_s[b] = u[tc - SUBLANES:tc]
        a, bi = _rg_gate_math(uc, wa_ref, wx_ref, ba_ref, bx_ref, lam_ref)
        for l in range(nslab):
            a_s[l, pl.ds(b, tc, stride=nb), :] = a[:, l * LANES:(l + 1) * LANES]
            b_s[l, pl.ds(b, tc, stride=nb), :] = bi[:, l * LANES:(l + 1) * LANES]

    def step(t, hs):
        r0 = pl.multiple_of(t * nb, nb)
        new = []
        for l in range(nslab):
            h = a_s[l, pl.ds(r0, nb), :] * hs[l] + b_s[l, pl.ds(r0, nb), :]
            b_s[l, pl.ds(r0, nb), :] = h
            new.append(h)
        return tuple(new)

    hs = lax.fori_loop(0, tc, step, tuple(h_s[l] for l in range(nslab)), unroll=8)
    for l in range(nslab):
        h_s[l] = hs[l]
        hl_ref[:, l * LANES:(l + 1) * LANES] = hs[l]

    for b in range(nb):
        gate = _gelu(g_ref[b])
        for l in range(nslab):
            hb = b_s[l, pl.ds(b, tc, stride=nb), :]
            y_ref[b, :, l * LANES:(l + 1) * LANES] = (hb * gate[:, l * LANES:(l + 1) * LANES]).astype(y_ref.dtype)


def _rglru_seq(ug, cw, cb, wa_bd, wx_bd, ba, bx, lam, tc):
    nb, t, w2 = ug.shape
    w = w2 // 2
    assert nb == SUBLANES and t % tc == 0 and w % LANES == 0
    row = lambda a: a.reshape(1, w)
    return pl.pallas_call(
        _rglru_seq_kernel,
        grid=(t // tc,),
        in_specs=[pl.BlockSpec((nb, tc, w), lambda i: (0, i, 0)), pl.BlockSpec((nb, tc, w), lambda i: (0, i, 1)),
                  _const_spec(cw.shape), _const_spec((1, w)), _const_spec((w, w)), _const_spec((w, w)),
                  _const_spec((1, w)), _const_spec((1, w)), _const_spec((1, w))],
        out_specs=[pl.BlockSpec((nb, tc, w), lambda i: (0, i, 0)), pl.BlockSpec((nb, w), lambda i: (0, 0))],
        out_shape=[jax.ShapeDtypeStruct((nb, t, w), BF16), jax.ShapeDtypeStruct((nb, w), F32)],
        scratch_shapes=[pltpu.VMEM((w // LANES, tc * nb, LANES), F32), pltpu.VMEM((w // LANES, tc * nb, LANES), F32),
                        pltpu.VMEM((nb, SUBLANES, w), F32), pltpu.VMEM((w // LANES, nb, LANES), F32)],
        compiler_params=_cparams("arbitrary"), name="rglru_seq",
    )(ug, ug, cw, row(cb), wa_bd, wx_bd, row(ba), row(bx), row(lam))


def _rglru_step_kernel(u_ref, g_ref, cp_ref, h_ref, cw_ref, cb_ref, wa_ref, wx_ref, ba_ref, bx_ref, lam_ref,
                       y_ref, hn_ref):
    w = u_ref.shape[1]
    u = u_ref[...]
    uc = (cp_ref[:, 0:w] * cw_ref[0:1, :] + cp_ref[:, w:2 * w] * cw_ref[1:2, :]
          + cp_ref[:, 2 * w:3 * w] * cw_ref[2:3, :] + u * cw_ref[3:4, :] + cb_ref[...])
    a, bi = _rg_gate_math(uc, wa_ref, wx_ref, ba_ref, bx_ref, lam_ref)
    h = a * h_ref[...] + bi
    hn_ref[...] = h
    y_ref[...] = (h * _gelu(g_ref[...])).astype(y_ref.dtype)


def _rglru_step(ug, conv_prev, h_prev, cw, cb, wa_bd, wx_bd, ba, bx, lam):
    nb, w2 = ug.shape
    w = w2 // 2
    row = lambda a: a.reshape(1, w)
    full = lambda shape: pl.BlockSpec(shape, lambda i: (0,) * len(shape))
    return pl.pallas_call(
        _rglru_step_kernel,
        grid=(1,),
        in_specs=[pl.BlockSpec((nb, w), lambda i: (0, 0)), pl.BlockSpec((nb, w), lambda i: (0, 1)),
                  full((nb, 3 * w)), full((nb, w)), full(cw.shape), full((1, w)), full((w, w)), full((w, w)),
                  full((1, w)), full((1, w)), full((1, w))],
        out_specs=[full((nb, w)), full((nb, w))],
        out_shape=[jax.ShapeDtypeStruct((nb, w), BF16), jax.ShapeDtypeStruct((nb, w), F32)],
        compiler_params=_cparams("arbitrary"), name="rglru_step",
    )(ug, ug, conv_prev.reshape(nb, 3 * w), h_prev, cw, row(cb), wa_bd, wx_bd, row(ba), row(bx), row(lam))


ONES_ROWS = 16
LOG2E = 1.4426950408889634
FIXED_MAX_GAP = 64.0


def _flash_t(s, m_old, acc, vext):
    m_new = jnp.maximum(m_old, jnp.max(s, axis=0, keepdims=True))
    p = jnp.exp2(s - m_new).astype(BF16)
    return m_new, jnp.exp2(m_old - m_new) * acc + _dot(vext, p)


def _moba_seq_kernel(q_ref, k_ref, vt_ref, o_ref, km_s, qt_s, bias_s, acc_s, kmax_s, *, nblk):
    i = pl.program_id(2)
    blk = MOBA_BLOCK
    hd = HEAD_DIM

    @pl.when(i == 0)
    def _():
        for n in range(nblk):
            km_s[n:n + 1, :] = jnp.mean(k_ref[0, n * blk:(n + 1) * blk, :].astype(F32), axis=0, keepdims=True)
        kf = k_ref[0].astype(F32)
        head_of_col = lax.broadcasted_iota(I32, (kf.shape[1], LANES), 0) // hd
        onehot = (head_of_col == lax.broadcasted_iota(I32, (kf.shape[1], LANES), 1)).astype(F32)
        kmax_s[...] = jnp.max(jnp.dot(kf * kf, onehot, precision=_HI, preferred_element_type=F32), axis=0, keepdims=True)

    tq = q_ref.shape[1]
    nqb = tq // blk
    hps = q_ref.shape[2] // hd
    qt = q_ref[0].T
    row_head = lax.broadcasted_iota(I32, (LANES, 1), 0) // hd
    blk_id = lax.broadcasted_iota(I32, (nblk, 1), 0)
    own_blk = i * nqb + lax.broadcasted_iota(I32, (1, tq), 1) // blk
    causal = jnp.where(lax.broadcasted_iota(I32, (blk, 1), 0) <= lax.broadcasted_iota(I32, (1, blk), 1), 0.0, NEG)
    ones = jnp.ones((ONES_ROWS, blk), BF16)
    pair = lambda a, hh: a[:, (hh // 2) * LANES:(hh // 2 + 1) * LANES]

    score_cap = []
    for hh in range(hps):
        qth =jnp.where(row_head == hh % 2, qt[(hh // 2) * LANES:(hh // 2 + 1) * LANES], 0.0)
        gate = jnp.dot(pair(km_s[...], hh), qth, precision=_HI, preferred_element_type=F32)
        gate = jnp.where(blk_id < own_blk, gate, -jnp.inf)
        rank = jnp.zeros(gate.shape, I32)
        for m in range(nblk):
            gm = gate[m:m + 1, :]
            rank = rank + jnp.where(gm > gate, 1, jnp.where(gm == gate, (m < blk_id).astype(I32), 0))
        bias_s[hh] = jnp.where(jnp.where(blk_id < own_blk, rank, MOBA_TOPK) < MOBA_TOPK, 0.0, NEG)
        qs = qth * (hd ** -0.5 * LOG2E)
        qt_s[hh] = qs.astype(BF16)
        acc_s[hh] = jnp.zeros(acc_s.shape[1:], F32)
        score_cap.append(jnp.sqrt(jnp.sum(qs * qs, axis=0, keepdims=True)) * (jnp.sqrt(kmax_s[:, hh:hh + 1]) * 1.01))

    def key_block(n, ms, mask_of, running_max):
        n0 = pl.multiple_of(n * blk, blk)
        kt = k_ref[0, pl.ds(n0, blk), :]
        scores = lambda hh: _dot(pair(kt, hh), qt_s[hh]) + mask_of(hh)
        s_next = scores(0)
        out = []
        for hh in range(hps):
            s = s_next
            if hh + 1 < hps:
                s_next = scores(hh + 1)
            vext = jnp.concatenate([vt_ref[0, hh * hd:(hh + 1) * hd, pl.ds(n0, blk)], ones], axis=0)
            if running_max:
                m_new, acc_s[hh] = _flash_t(s, ms[hh], acc_s[hh], vext)
            else:
                m_new = ms[hh]
                acc_s[hh] = acc_s[hh] + _dot(vext, jnp.exp2(s - m_new).astype(BF16))
            out.append(m_new)
        return tuple(out)

    ms = tuple(jnp.full((1, tq), NEG, F32) for _ in range(hps))
    for c in range(nqb):
        n = i * nqb + c

        def mask_of(hh, c=c, n=n):
            row = bias_s[hh, pl.ds(n, 1), :]
            parts = [jnp.full((blk, blk), NEG, F32) if a < c else causal if a == c else
                     jnp.broadcast_to(row[:, a * blk:(a + 1) * blk], (blk, blk)) for a in range(nqb)]
            return jnp.concatenate(parts, axis=1)

        ms = key_block(n, ms, mask_of, True)
    past = lambda running_max: lax.fori_loop(
        0, i * nqb, lambda n, ms: key_block(n, ms, lambda hh: bias_s[hh, pl.ds(n, 1), :], running_max), ms)
    gap = jnp.max(jnp.concatenate([score_cap[hh] - ms[hh] for hh in range(hps)], axis=0))
    lax.cond(gap < FIXED_MAX_GAP, lambda: past(False), lambda: past(True))
    ot = jnp.concatenate([acc_s[hh, 0:hd] / acc_s[hh, hd:hd + 1] for hh in range(hps)], axis=0)
    o_ref[0] = ot.T.astype(o_ref.dtype)


def _moba_seq(q, kb, vtb):
    nb, t, hw = q.shape
    blk = MOBA_BLOCK
    hps, nqb = MOBA_HEADS_PER_STEP, MOBA_QBLOCKS_PER_STEP
    gw, tq = hps * HEAD_DIM, nqb * blk
    assert t % tq == 0 and hw % gw == 0 and 2 * HEAD_DIM == LANES and hps % 2 == 0
    nblk = t // blk
    return pl.pallas_call(
        functools.partial(_moba_seq_kernel, nblk=nblk),
        grid=(nb, hw // gw, t // tq),
        in_specs=[pl.BlockSpec((1, tq, gw), lambda b, h, i: (b, i, h)),
                  pl.BlockSpec((1, t, gw), lambda b, h, i: (b, 0, h)),
                  pl.BlockSpec((1, gw, t), lambda b, h, i: (b, h, 0))],
        out_specs=pl.BlockSpec((1, tq, gw), lambda b, h, i: (b, i, h)),
        out_shape=jax.ShapeDtypeStruct((nb, t, hw), BF16),
        scratch_shapes=[pltpu.VMEM((nblk, gw), F32), pltpu.VMEM((hps, LANES, tq), BF16),
                        pltpu.VMEM((hps, nblk, tq), F32), pltpu.VMEM((hps, HEAD_DIM + ONES_ROWS, tq), F32),
                        pltpu.VMEM((1, LANES), F32)],
        compiler_params=_cparams("parallel", "parallel", "arbitrary"), name="moba_seq",
    )(q, kb, vtb)


def _dsa_seq_kernel(qi_ref, kiw_ref, ki_ref, q_ref, k_ref, vt_ref, o_ref, key_s, qit_s, qt_s, acc_s, kmax_s,
                    *, tk, nsel, idx_bits):
    i = pl.program_id(1)
    tq = qi_ref.shape[1]
    nkv, hd = DSA_KV_HEADS, HEAD_DIM
    grp = q_ref.shape[2] // hd // nkv
    nck = ((i + 1) * tq + tk - 1) // tk
    qpos = i * tq + lax.broadcasted_iota(I32, (1, tq), 1)
    krow = lax.broadcasted_iota(I32, (tk, 1), 0)

    qit = (qi_ref[0] * (IDX_DIM ** -0.5)).T.astype(BF16)
    qit_s[...] = jnp.concatenate([qit[h * IDX_DIM:(h + 1) * IDX_DIM] for h in range(IDX_HEADS)], axis=1)
    wt = kiw_ref[0].T[IDX_DIM:IDX_DIM + IDX_HEADS] * (IDX_HEADS ** -0.5)
    qtf = (q_ref[0] * (hd ** -0.5 * LOG2E)).T
    qt = qtf.astype(BF16)
    zero = jnp.zeros((hd, grp * tq), BF16)
    heads_of = lambda a, n: [a[(n * grp + g) * hd:(n * grp + g + 1) * hd] for g in range(grp)]
    for n in range(nkv):
        own = jnp.concatenate(heads_of(qt, n), axis=1)
        qt_s[n] = jnp.concatenate([own, zero] if n % 2 == 0 else [zero, own], axis=0)

    @pl.when(i == 0)
    def _():
        kf = k_ref[0].astype(F32)
        head_of_col = lax.broadcasted_iota(I32, (kf.shape[1], LANES), 0) // hd
        onehot = (head_of_col == lax.broadcasted_iota(I32, (kf.shape[1], LANES), 1)).astype(F32)
        norms2 = jnp.dot(kf * kf, onehot, precision=_HI, preferred_element_type=F32)
        kmax_s[...] = jnp.max(norms2, axis=0, keepdims=True)

    score_cap = [jnp.concatenate([jnp.sqrt(jnp.sum(h * h, axis=0, keepdims=True)) for h in heads_of(qtf, n)], axis=1)
                 * (jnp.sqrt(kmax_s[:, n:n + 1]) * 1.01) for n in range(nkv)]

    def score_chunk(c, carry):
        c0 = pl.multiple_of(c * tk, tk)
        d = _dot(ki_ref[0, pl.ds(c0, tk), :], qit_s[...])
        sc = jnp.zeros((tk, tq), F32)
        for h in range(IDX_HEADS):
            sc = sc + wt[h:h + 1, :] * jnp.maximum(d[:, h * tq:(h + 1) * tq], 0.0)
        sc = jnp.where(c0 + krow <= qpos, sc, -jnp.inf)
        key_s[pl.ds(c0, tk), :] = _sortable(sc)
        return carry

    lax.fori_loop(0, nck, score_chunk, 0)

    def count(pred):
        def body(c, acc):
            c0 = pl.multiple_of(c * tk, tk)
            hit = pred(key_s[pl.ds(c0, tk), :], c0)
            return acc + jnp.sum(hit.reshape(tk // SUBLANES, SUBLANES, tq), axis=0)
        acc = lax.fori_loop(0, nck, body, jnp.zeros((SUBLANES, tq), I32))
        return jnp.sum(acc, axis=0, keepdims=True)

    def value_bit(it, thr):
        cand = thr + jnp.left_shift(jnp.int32(1), 31 - it)
        cnt = count(lambda key, c0: jnp.where(key >= cand, 1, 0))
        return jnp.where(cnt >= nsel, cand, thr)

    thr = lax.fori_loop(0, 32, value_bit, jnp.full((1, tq), INT_MIN, I32))
    need = nsel - count(lambda key, c0: jnp.where(key > thr, 1, 0))
    ties = count(lambda key, c0: jnp.where(key == thr, 1, 0))

    def index_bit(it, cut):
        cand = cut + jnp.left_shift(jnp.int32(1), idx_bits - 1 - it)
        cnt = count(lambda key, c0: jnp.where(key == thr, jnp.where(c0 + krow < cand, 1, 0), 0))
        return jnp.where(cnt < need, cand, cut)

    cut = lax.cond(jnp.max(ties - need) > 0,
                   lambda: lax.fori_loop(0, idx_bits, index_bit, jnp.zeros((1, tq), I32)),
                   lambda: jnp.full((1, tq), 2 ** idx_bits, I32))

    acc_s[...] = jnp.zeros(acc_s.shape, F32)
    ones = jnp.ones((ONES_ROWS, tk), BF16)

    def attend(c, ms, running_max):
        c0 = pl.multiple_of(c * tk, tk)
        key = key_s[pl.ds(c0, tk), :]
        kpos = c0 + krow
        take = jnp.where(key > thr, 1, jnp.where(key == thr, jnp.where(kpos <= cut, 1, 0), 0))
        take = jnp.where(kpos <= qpos, take, 0)
        bias = jnp.where(take > 0, 0.0, NEG)
        bias = jnp.concatenate([bias] * grp, axis=1)
        kc = k_ref[0, pl.ds(c0, tk), :]
        scores = lambda n: _dot(kc[:, (n // 2) * LANES:(n // 2 + 1) * LANES], qt_s[n]) + bias
        s_next = scores(0)
        out = []
        for n in range(nkv):
            s = s_next
            if n + 1 < nkv:
                s_next = scores(n + 1)
            vext = jnp.concatenate([vt_ref[0, n * hd:(n + 1) * hd, pl.ds(c0, tk)], ones], axis=0)
            if running_max:
                m_new, acc_s[n] = _flash_t(s, ms[n], acc_s[n], vext)
            else:
                m_new = ms[n]
                acc_s[n] = acc_s[n] + _dot(vext, jnp.exp2(s - m_new).astype(BF16))
            out.append(m_new)
        return tuple(out)

    ms = attend(0, tuple(jnp.full((1, grp * tq), NEG, F32) for _ in range(nkv)), True)
    gap = jnp.max(jnp.concatenate([score_cap[n] - ms[n] for n in range(nkv)], axis=0))
    lax.cond(gap < FIXED_MAX_GAP,
             lambda: lax.fori_loop(1, nck, lambda c, ms: attend(c, ms, False), ms),
             lambda: lax.fori_loop(1, nck, lambda c, ms: attend(c, ms, True), ms))
    pieces = []
    for n in range(nkv):
        on = acc_s[n, 0:hd] / acc_s[n, hd:hd + 1]
        pieces += [on[:, g * tq:(g + 1) * tq] for g in range(grp)]
    o_ref[0] = jnp.concatenate(pieces, axis=0).T.astype(o_ref.dtype)


def _dsa_seq(qi, kiw, kib, q, kb, vtb, tq, tk):
    nb, t, qw = q.shape
    assert t % tq == 0 and t % tk == 0 and tq == LANES and 2 * HEAD_DIM == LANES and DSA_KV_HEADS % 2 == 0
    nsel = min(DSA_TOPK, t // 4)
    kvw = kb.shape[2]
    grp = qw // HEAD_DIM // DSA_KV_HEADS
    return pl.pallas_call(
        functools.partial(_dsa_seq_kernel, tk=tk, nsel=nsel, idx_bits=max(1, (t - 1).bit_length())),
        grid=(nb, t // tq),
        in_specs=[pl.BlockSpec((1, tq, qi.shape[2]), lambda b, i: (b, i, 0)),
                  pl.BlockSpec((1, tq, LANES), lambda b, i: (b, i, 0)),
                  pl.BlockSpec((1, t, IDX_DIM), lambda b, i: (b, 0, 0)),
                  pl.BlockSpec((1, tq, qw), lambda b, i: (b, i, 0)),
                  pl.BlockSpec((1, t, kvw), lambda b, i: (b, 0, 0)),
                  pl.BlockSpec((1, kvw, t), lambda b, i: (b, 0, 0))],
        out_specs=pl.BlockSpec((1, tq, qw), lambda b, i: (b, i, 0)),
        out_shape=jax.ShapeDtypeStruct((nb, t, qw), BF16),
        scratch_shapes=[pltpu.VMEM((t, tq), I32), pltpu.VMEM((IDX_DIM, IDX_HEADS * tq), BF16),
                        pltpu.VMEM((DSA_KV_HEADS, LANES, grp * tq), BF16),
                        pltpu.VMEM((DSA_KV_HEADS, HEAD_DIM + ONES_ROWS, grp * tq), F32), pltpu.VMEM((1, LANES), F32)],
        compiler_params=_cparams("parallel", "arbitrary"), name="dsa_seq",
    )(qi, kiw, kib, q, kb, vtb)


def _ffn_core(x1, gn_ref, wg_ref, wu_ref, cw_ref, cb_ref, wd_ref, conv_fn, fc):
    xn = _rms(x1, gn_ref[...]).astype(BF16)
    dff = wg_ref.shape[1]
    acc = jnp.zeros(x1.shape, F32)
    for c in range(dff // fc):
        cs = slice(c * fc, (c + 1) * fc)
        g = _dot(xn, wg_ref[:, cs])
        u = _dot(xn, wu_ref[:, cs])
        gm2, gm1 = conv_fn(g, cs)
        gc = gm2 * cw_ref[0:1, cs] + gm1 * cw_ref[1:2, cs] + g * cw_ref[2:3, cs] + cb_ref[:, cs]
        acc = acc + _dot((_gelu(gc) * u).astype(BF16), wd_ref[cs, :])
    return x1 + acc


def _ffn_seq_kernel(*refs, n_y, tiles_per_seq, final, fc):
    x_ref = refs[0]
    y_refs = refs[1:1 + n_y]
    wo_refs = refs[1 + n_y:1 + 2 * n_y]
    gn_ref, wg_ref, wu_ref, cw_ref, cb_ref, wd_ref, gf_ref = refs[1 + 2 * n_y:8 + 2 * n_y]
    out_ref, st_ref, tail_s = refs[8 + 2 * n_y:]
    tm = x_ref.shape[0]

    @pl.when(pl.program_id(0) % tiles_per_seq == 0)
    def _():
        tail_s[...] = jnp.zeros_like(tail_s)

    x1 = x_ref[...]
    for y_ref, wo_ref in zip(y_refs, wo_refs):
        x1 = x1 + _dot(y_ref[...], wo_ref[...])

    def conv_fn(g, cs):
        prev = tail_s[:, cs]
        tail_s[:, cs] = g[tm - SUBLANES:tm]
        return _shift_rows(g, prev, 2), _shift_rows(g, prev, 1)

    x2 = _ffn_core(x1, gn_ref, wg_ref, wu_ref, cw_ref, cb_ref, wd_ref, conv_fn, fc)
    st_ref[0] = tail_s[...]
    out_ref[...] = _rms(x2, gf_ref[...]) if final else x2


def _ffn_step_kernel(*refs, n_y, final, fc):
    x_ref = refs[0]
    y_refs = refs[1:1 + n_y]
    wo_refs = refs[1 + n_y:1 + 2 * n_y]
    gn_ref, wg_ref, wu_ref, cw_ref, cb_ref, wd_ref, gf_ref, p_ref = refs[1 + 2 * n_y:9 + 2 * n_y]
    out_ref, g_ref = refs[9 + 2 * n_y:]
    dff = wg_ref.shape[1]

    x1 = x_ref[...]
    for y_ref, wo_ref in zip(y_refs, wo_refs):
        x1 = x1 + _dot(y_ref[...], wo_ref[...])

    def conv_fn(g, cs):
        g_ref[:, cs] = g
        return p_ref[:, cs], p_ref[:, dff + cs.start:dff + cs.stop]

    x2 = _ffn_core(x1, gn_ref, wg_ref, wu_ref, cw_ref, cb_ref, wd_ref, conv_fn, fc)
    out_ref[...] = _rms(x2, gf_ref[...]) if final else x2


def _ffn_weights_specs(d, dff):
    return [_const_spec((1, d)), _const_spec((d, dff)), _const_spec((d, dff)), _const_spec((3, dff)),
            _const_spec((1, dff)), _const_spec((dff, d)), _const_spec((1, d))]


def _ffn_seq(x2d, ys, wos, gn, wg, wu, cw, cb, wd, gfinal, final, seq_len, tm, fc=1024):
    n, d = x2d.shape
    dff = wg.shape[1]
    assert seq_len % tm == 0 and dff % fc == 0
    tps = seq_len // tm
    row_spec = lambda c: pl.BlockSpec((tm, c), lambda i: (i, 0))
    return pl.pallas_call(
        functools.partial(_ffn_seq_kernel, n_y=len(ys), tiles_per_seq=tps, final=final, fc=fc),
        grid=(n // tm,),
        in_specs=[row_spec(d)] + [row_spec(y.shape[1]) for y in ys] + [_const_spec(w.shape) for w in wos]
                 + _ffn_weights_specs(d, dff),
        out_specs=[row_spec(d), pl.BlockSpec((1, SUBLANES, dff), lambda i: (i // tps, 0, 0))],
        out_shape=[jax.ShapeDtypeStruct((n, d), F32), jax.ShapeDtypeStruct((n // seq_len, SUBLANES, dff), F32)],
        scratch_shapes=[pltpu.VMEM((SUBLANES, dff), F32)],
        compiler_params=_cparams("arbitrary"), name="ffn_seq",
    )(x2d, *ys, *wos, gn.reshape(1, d), wg, wu, cw, cb.reshape(1, dff), wd, gfinal.reshape(1, d))


def _ffn_step(x2d, ys, wos, gn, wg, wu, cw, cb, wd, gfinal, final, prev, fc=1024):
    n, d = x2d.shape
    dff = wg.shape[1]
    full = lambda a: pl.BlockSpec(a.shape, lambda i: (0,) * a.ndim)
    prev2 = prev.reshape(n, 2 * dff)
    return pl.pallas_call(
        functools.partial(_ffn_step_kernel, n_y=len(ys), final=final, fc=fc),
        grid=(1,),
        in_specs=[full(x2d)] + [full(y) for y in ys] + [_const_spec(w.shape) for w in wos]
                 + _ffn_weights_specs(d, dff) + [full(prev2)],
        out_specs=[pl.BlockSpec((n, d), lambda i: (0, 0)), pl.BlockSpec((n, dff), lambda i: (0, 0))],
        out_shape=[jax.ShapeDtypeStruct((n, d), F32), jax.ShapeDtypeStruct((n, dff), F32)],
        compiler_params=_cparams("arbitrary"), name="ffn_step",
    )(x2d, *ys, *wos, gn.reshape(1, d), wg, wu, cw, cb.reshape(1, dff), wd, gfinal.reshape(1, d), prev2)


def _moba_pick_kernel(pt_ref, q_ref, *refs, pages_per_step):
    page_refs = refs[:pages_per_step]
    sel_ref, tv_s, ti_s = refs[pages_per_step:]
    s = pl.program_id(1)
    ppb = MOBA_BLOCK // PAGE_SIZE
    bps = pages_per_step // ppb
    heads = q_ref.shape[1]

    @pl.when(s == 0)
    def _():
        tv_s[...] = jnp.full(tv_s.shape, -jnp.inf, F32)
        ti_s[...] = jnp.zeros(ti_s.shape, I32)

    q = q_ref[0]
    t1, t2, t3 = tv_s[0], tv_s[1], tv_s[2]
    i1, i2, i3 = ti_s[0], ti_s[1], ti_s[2]
    for j in range(bps):
        tot = jnp.zeros((heads, PAGE_SIZE), F32)
        for r in range(ppb):
            tot = tot + jnp.sum(page_refs[j * ppb + r][0] * q, axis=1)
        g = jnp.sum(tot, axis=1, keepdims=True) * (1.0 / MOBA_BLOCK)
        n = s * bps + j
        c1, c2, c3 = g > t1, g > t2, g > t3
        t3, i3 = jnp.where(c2, t2, jnp.where(c3, g, t3)), jnp.where(c2, i2, jnp.where(c3, n, i3))
        t2, i2 = jnp.where(c1, t1, jnp.where(c2, g, t2)), jnp.where(c1, i1, jnp.where(c2, n, i2))
        t1, i1 = jnp.where(c1, g, t1), jnp.where(c1, n, i1)
    tv_s[0], tv_s[1], tv_s[2] = t1, t2, t3
    ti_s[0], ti_s[1], ti_s[2] = i1, i2, i3

    @pl.when(s == pl.num_programs(1) - 1)
    def _():
        lane = lax.broadcasted_iota(I32, (heads, LANES), 1)
        sel_ref[0] = jnp.where(lane == 0, i1, jnp.where(lane == 1, i2, i3))


def _moba_pick(page_table, q, cache_kt, pages_per_step=PAGES_PER_STEP):
    nb, npg = page_table.shape
    _, heads, hd, _ = cache_kt.shape
    ppb = MOBA_BLOCK // PAGE_SIZE
    assert MOBA_TOPK == 3 and npg % pages_per_step == 0 and pages_per_step % ppb == 0 and npg // ppb >= MOBA_TOPK
    page_spec = lambda j: pl.BlockSpec((1, heads, hd, PAGE_SIZE),
                                       lambda b, s, pt: (pt[b, s * pages_per_step + j], 0, 0, 0))
    grid_spec = pltpu.PrefetchScalarGridSpec(
        num_scalar_prefetch=1, grid=(nb, npg // pages_per_step),
        in_specs=[pl.BlockSpec((1, heads, hd, 1), lambda b, s, pt: (b, 0, 0, 0))]
                 + [page_spec(j) for j in range(pages_per_step)],
        out_specs=pl.BlockSpec((1, heads, LANES), lambda b, s, pt: (b, 0, 0)),
        scratch_shapes=[pltpu.VMEM((MOBA_TOPK, heads, 1), F32), pltpu.VMEM((MOBA_TOPK, heads, 1), I32)])
    return pl.pallas_call(
        functools.partial(_moba_pick_kernel, pages_per_step=pages_per_step),
        grid_spec=grid_spec, out_shape=jax.ShapeDtypeStruct((nb, heads, LANES), I32),
        compiler_params=_cparams("parallel", "arbitrary"), name="moba_pick",
    )(page_table, q.reshape(nb, heads, hd, 1), *([cache_kt] * pages_per_step))


def _moba_step_kernel(sel_ref, pt_ref, q_ref, kn_ref, vn_ref, *refs, npage):
    k_refs = refs[:npage]
    v_refs = refs[npage:2 * npage]
    o_ref = refs[2 * npage]
    hd = q_ref.shape[3]
    q8 = jnp.broadcast_to(q_ref[0, 0] * (HEAD_DIM ** -0.5), (SUBLANES, hd))
    s_new = jnp.sum(q8 * kn_ref[0, 0], axis=1, keepdims=True)
    qb = q8.astype(BF16)
    s = jnp.concatenate([_dot(qb, k_refs[j][0, 0].astype(BF16)) for j in range(npage)], axis=1)
    m = jnp.maximum(s_new, jnp.max(s, axis=1, keepdims=True))
    p_new = jnp.exp(s_new - m)
    p = jnp.exp(s - m)
    l = p_new + jnp.sum(p, axis=1, keepdims=True)
    pb = p.astype(BF16)
    acc = p_new * vn_ref[0, 0]
    for j in range(npage):
        acc = acc + _dot_t(pb[:, j * PAGE_SIZE:(j + 1) * PAGE_SIZE], v_refs[j][0, 0].astype(BF16))
    o_ref[0, 0] = (acc / l)[0:1]


def _moba_step(sel, page_table, q, k_new, v_new, cache_kt, cache_vt):
    nb, heads, hd = q.shape
    ppb = MOBA_BLOCK // PAGE_SIZE
    npage = MOBA_TOPK * ppb

    def page_spec(j):
        r, pg = divmod(j, ppb)
        return pl.BlockSpec((1, 1, hd, PAGE_SIZE),
                            lambda b, h, sel, pt: (pt[b, sel[b, h * MOBA_TOPK + r] * ppb + pg], h, 0, 0))

    vec = pl.BlockSpec((1, 1, 1, hd), lambda b, h, sel, pt: (b, h, 0, 0))
    grid_spec = pltpu.PrefetchScalarGridSpec(
        num_scalar_prefetch=2, grid=(nb, heads),
        in_specs=[vec, vec, vec] + [page_spec(j) for j in range(npage)] * 2,
        out_specs=vec)
    r4 = lambda a: a.reshape(nb, heads, 1, hd)
    return pl.pallas_call(
        functools.partial(_moba_step_kernel, npage=npage),
        grid_spec=grid_spec, out_shape=jax.ShapeDtypeStruct((nb, heads, 1, hd), F32),
        compiler_params=_cparams("parallel", "arbitrary"), name="moba_step",
    )(sel, page_table, r4(q), r4(k_new), r4(v_new), *([cache_kt] * npage), *([cache_vt] * npage)).reshape(nb, heads * hd)


def _dsa_score_kernel(pt_ref, qi_ref, w_ref, *refs, pages_per_step):
    page_refs = refs[:pages_per_step]
    sc_ref = refs[pages_per_step]
    qi = qi_ref[0] * (IDX_DIM ** -0.5)
    w = w_ref[0] * (IDX_HEADS ** -0.5)
    for j in range(pages_per_step):
        d = _dot(qi.astype(BF16), page_refs[j][0].astype(BF16))
        sc_ref[0, j:j + 1, :] = jnp.sum(w * jnp.maximum(d, 0.0), axis=0, keepdims=True)


def _dsa_score(page_table, qi, wi, cache_ki, pages_per_step=PAGES_PER_STEP):
    nb, npg = page_table.shape
    assert npg % pages_per_step == 0
    page_spec = lambda j: pl.BlockSpec((1, IDX_DIM, PAGE_SIZE), lambda b, s, pt: (pt[b, s * pages_per_step + j], 0, 0))
    grid_spec = pltpu.PrefetchScalarGridSpec(
        num_scalar_prefetch=1, grid=(nb, npg // pages_per_step),
        in_specs=[pl.BlockSpec((1, IDX_HEADS, IDX_DIM), lambda b, s, pt: (b, 0, 0)),
                  pl.BlockSpec((1, IDX_HEADS, 1), lambda b, s, pt: (b, 0, 0))]
                 + [page_spec(j) for j in range(pages_per_step)],
        out_specs=pl.BlockSpec((1, pages_per_step, PAGE_SIZE), lambda b, s, pt: (b, s, 0)))
    return pl.pallas_call(
        functools.partial(_dsa_score_kernel, pages_per_step=pages_per_step),
        grid_spec=grid_spec, out_shape=jax.ShapeDtypeStruct((nb, npg, PAGE_SIZE), F32),
        compiler_params=_cparams("parallel", "arbitrary"), name="dsa_score",
    )(page_table, qi, wi, *([cache_ki] * pages_per_step))


def _dsa_pick_kernel(sc_ref, qi_ref, w_ref, kin_ref, bias_ref, bnew_ref, *, nsel, idx_bits):
    nb, npg, _ = sc_ref.shape
    d = jnp.sum(qi_ref[...] * (IDX_DIM ** -0.5) * kin_ref[...], axis=1, keepdims=True)
    per_head = (w_ref[...] * (IDX_HEADS ** -0.5) * jnp.maximum(d, 0.0)).reshape(nb, IDX_HEADS, 1)
    key_new = _sortable(jnp.sum(per_head, axis=1, keepdims=True))
    key = _sortable(sc_ref[...])
    pos = (lax.broadcasted_iota(I32, key.shape, 1) * PAGE_SIZE + lax.broadcasted_iota(I32, key.shape, 2))
    pos_new = npg * PAGE_SIZE

    def total(hit, hit_new):
        return jnp.sum(jnp.sum(hit, axis=1, keepdims=True), axis=2, keepdims=True) + hit_new

    def value_bit(it, thr):
        cand = thr + jnp.left_shift(jnp.int32(1), 31 - it)
        cnt = total(jnp.where(key >= cand, 1, 0), jnp.where(key_new >= cand, 1, 0))
        return jnp.where(cnt >= nsel, cand, thr)

    thr = lax.fori_loop(0, 32, value_bit, jnp.full((nb, 1, 1), INT_MIN, I32))
    need = nsel - total(jnp.where(key > thr, 1, 0), jnp.where(key_new > thr, 1, 0))

    def index_bit(it, cut):
        cand = cut + jnp.left_shift(jnp.int32(1), idx_bits - 1 - it)
        cnt = total(jnp.where(key == thr, jnp.where(pos < cand, 1, 0), 0),
                    jnp.where(key_new == thr, jnp.where(pos_new < cand, 1, 0), 0))
        return jnp.where(cnt < need, cand, cut)

    cut = lax.fori_loop(0, idx_bits, index_bit, jnp.zeros((nb, 1, 1), I32))
    take = jnp.where(key > thr, 1, jnp.where(key == thr, jnp.where(pos <= cut, 1, 0), 0))
    bias_ref[...] = jnp.where(take > 0, 0.0, NEG)
    take_new = jnp.where(key_new > thr, 1, jnp.where(key_new == thr, jnp.where(pos_new <= cut, 1, 0), 0))
    bnew_ref[...] = jnp.broadcast_to(jnp.where(take_new > 0, 0.0, NEG), bnew_ref.shape)


def _dsa_pick(scores, qi, wi, ki_new):
    nb, npg, _ = scores.shape
    total_len = npg * PAGE_SIZE + 1
    rows = nb * IDX_HEADS
    full = lambda shape: pl.BlockSpec(shape, lambda i: (0,) * len(shape))
    return pl.pallas_call(
        functools.partial(_dsa_pick_kernel, nsel=min(DSA_TOPK, total_len // 4),
                          idx_bits=max(1, (total_len - 1).bit_length())),
        grid=(1,),
        in_specs=[full(scores.shape), full((rows, IDX_DIM)), full((rows, 1)), full((rows, IDX_DIM))],
        out_specs=[full(scores.shape), full((nb, 1, LANES))],
        out_shape=[jax.ShapeDtypeStruct(scores.shape, F32), jax.ShapeDtypeStruct((nb, 1, LANES), F32)],
        compiler_params=_cparams("arbitrary"), name="dsa_pick",
    )(scores, qi.reshape(rows, IDX_DIM), wi.reshape(rows, 1), jnp.repeat(ki_new, IDX_HEADS, axis=0))


def _dsa_step_kernel(pt_ref, bias_ref, bnew_ref, q_ref, kn_ref, vn_ref, *refs, pages_per_step):
    k_refs = refs[:pages_per_step]
    v_refs = refs[pages_per_step:2 * pages_per_step]
    o_ref, m_s, l_s, acc_s = refs[2 * pages_per_step:]
    s = pl.program_id(1)
    nh = q_ref.shape[1]
    kvw = kn_ref.shape[2]
    grp = nh // DSA_KV_HEADS
    row_kv = lax.broadcasted_iota(I32, (nh, kvw), 0) // grp
    lane_kv = lax.broadcasted_iota(I32, (nh, kvw), 1) // HEAD_DIM
    own = row_kv == lane_kv
    q = q_ref[0] * (HEAD_DIM ** -0.5)
    qbd = jnp.where(own, jnp.concatenate([q] * DSA_KV_HEADS, axis=1), 0.0)

    @pl.when(s == 0)
    def _():
        m_s[...] = jnp.sum(qbd * kn_ref[0], axis=1, keepdims=True) + bnew_ref[0][:, 0:1]
        l_s[...] = jnp.ones_like(l_s)
        acc_s[...] = jnp.broadcast_to(vn_ref[0], acc_s.shape)

    m, l, acc = m_s[...], l_s[...], acc_s[...]
    qb = qbd.astype(BF16)
    sc = jnp.concatenate(
        [_dot(qb, k_refs[j][0].reshape(kvw, PAGE_SIZE).astype(BF16)) + bias_ref[0, pl.ds(s * pages_per_step + j, 1), :]
         for j in range(pages_per_step)], axis=1)
    m_new = jnp.maximum(m, jnp.max(sc, axis=1, keepdims=True))
    alpha = jnp.exp(m - m_new)
    p = jnp.exp(sc - m_new)
    l = alpha * l + jnp.sum(p, axis=1, keepdims=True)
    pb = p.astype(BF16)
    acc = alpha * acc
    for j in range(pages_per_step):
        vt = v_refs[j][0].reshape(kvw, PAGE_SIZE).astype(BF16)
        acc = acc + _dot_t(pb[:, j * PAGE_SIZE:(j + 1) * PAGE_SIZE], vt)
    m = m_new
    m_s[...], l_s[...], acc_s[...] = m, l, acc

    @pl.when(s == pl.num_programs(1) - 1)
    def _():
        on = jnp.where(own, acc / l, 0.0)
        o = on[:, 0:HEAD_DIM]
        for n in range(1, DSA_KV_HEADS):
            o = o + on[:, n * HEAD_DIM:(n + 1) * HEAD_DIM]
        o_ref[0] = o


def _dsa_step(page_table, bias, bias_new, q, k_new, v_new, cache_k, cache_v, pages_per_step=PAGES_PER_STEP):
    nb, npg = page_table.shape
    nh = q.shape[1]
    kvw = k_new.shape[1]
    bmap = lambda b, s, pt: (b, 0, 0)
    page_spec = lambda j: pl.BlockSpec((1, DSA_KV_HEADS, HEAD_DIM, PAGE_SIZE),
                                       lambda b, s, pt: (pt[b, s * pages_per_step + j], 0, 0, 0))
    grid_spec = pltpu.PrefetchScalarGridSpec(
        num_scalar_prefetch=1, grid=(nb, npg // pages_per_step),
        in_specs=[pl.BlockSpec((1, npg, PAGE_SIZE), bmap), pl.BlockSpec((1, 1, LANES), bmap),
                  pl.BlockSpec((1, nh, HEAD_DIM), bmap), pl.BlockSpec((1, 1, kvw), bmap), pl.BlockSpec((1, 1, kvw), bmap)]
                 + [page_spec(j) for j in range(pages_per_step)] * 2,
        out_specs=pl.BlockSpec((1, nh, HEAD_DIM), bmap),
        scratch_shapes=[pltpu.VMEM((nh, 1), F32), pltpu.VMEM((nh, 1), F32), pltpu.VMEM((nh, kvw), F32)])
    return pl.pallas_call(
        functools.partial(_dsa_step_kernel, pages_per_step=pages_per_step),
        grid_spec=grid_spec, out_shape=jax.ShapeDtypeStruct((nb, nh, HEAD_DIM), F32),
        compiler_params=_cparams("parallel", "arbitrary"), name="dsa_step",
    )(page_table, bias, bias_new, q, k_new.reshape(nb, 1, kvw), v_new.reshape(nb, 1, kvw),
      *([cache_k] * pages_per_step), *([cache_v] * pages_per_step))


def _in0_segs(w, step):
    rows = ((F32, "rows"),)
    if step:
        return ((0, 2 * w, None, 1.0, rows), (2 * w, w, "heads", 1.0, rows), (3 * w, w, "heads", 1.0, rows),
                (4 * w, w, None, 1.0, rows))
    return ((0, 2 * w, None, 1.0, rows),
            (2 * w, w, "heads", 1.0, rows),
            (3 * w, w, "heads", 1.0, ((F32, "cols"), (BF16, "rows"))),
            (4 * w, w, None, 1.0, ((F32, "cols"), (BF16, "cols"))))


def _in1_segs(qw, kvw, iw, step):
    rows = ((F32, "rows"),)
    last = qw + 2 * kvw + iw
    if step:
        return ((0, qw, "heads", 1.0, rows), (qw, kvw, "heads", 1.0, rows), (qw + kvw, kvw, None, 1.0, rows),
                (qw + 2 * kvw, iw, "heads", 1.0, rows), (last, LANES, "half", 1.0, rows + ((F32, "half_rows"),)))
    return ((0, qw, "heads", 1.0, rows),
            (qw, kvw, "heads", 1.0, ((F32, "cols"), (BF16, "rows"))),
            (qw + kvw, kvw, None, 1.0, ((F32, "cols"), (BF16, "cols"))),
            (qw + 2 * kvw, iw, "heads", 1.0, rows),
            (last, LANES, "half", 1.0, rows + ((F32, "half_cols"), (BF16, "half_rows"))))


def _prep_weights(p):
    d = p["w_in0"].shape[0]
    w = p["rg_lambda"].shape[0]
    nblocks = p["rg_gate_a_w"].shape[0]
    eye = jnp.eye(nblocks, dtype=F32)
    bd = lambda g: jnp.einsum("ncd,nm->ncmd", g, eye).reshape(w, w).astype(BF16)
    w_in1 = p["w_in1"]
    pad = (-w_in1.shape[1]) % LANES
    return dict(
        w_in0=p["w_in0"].astype(BF16), wa=bd(p["rg_gate_a_w"]), wx=bd(p["rg_gate_x_w"]),
        w_out0a=p["w_out0"][:w].astype(BF16), w_out0b=p["w_out0"][w:].astype(BF16),
        w_in1=jnp.pad(w_in1, ((0, 0), (0, pad))).astype(BF16), w_out1=p["w_out1"].astype(BF16),
        wg=p["ffn_w_gate"].astype(BF16), wu=p["ffn_w_up"].astype(BF16), wd=p["ffn_w_down"].astype(BF16))


def _prompt_group(x, p, wb):
    nb, t, d = x.shape
    n = nb * t
    w = p["rg_lambda"].shape[0]
    tm = ROW_TILE
    tabs = _rope_tables(jnp.arange(t, dtype=I32), t)
    x2 = x.reshape(n, d)

    ug, q0, k0t, k0b, v0t, v0tb = _norm_proj(x2, p["norm_mix"][0], wb["w_in0"], tabs, _in0_segs(w, False), tm,
                                             t // tm)
    ug3 = ug.reshape(nb, t, 2 * w)
    y_rg, h_last = _rglru_seq(ug3, p["rg_conv_w"], p["rg_conv_b"], wb["wa"], wb["wx"], p["rg_gate_a_b"],
                              p["rg_gate_x_b"], p["rg_lambda"], tc=SCAN_TILE)
    o0 = _moba_seq(q0.reshape(nb, t, w), k0b.reshape(nb, t, w), v0tb)
    x3, st0 = _ffn_seq(x2, [y_rg.reshape(n, w), o0.reshape(n, w)], [wb["w_out0a"], wb["w_out0b"]],
                       p["norm_ffn"][0], wb["wg"][0], wb["wu"][0], p["ffn_conv_w"][0], p["ffn_conv_b"][0],
                       wb["wd"][0], p["norm_final"], False, t, tm)

    qw = p["w_out1"].shape[0]
    kvw = DSA_KV_HEADS * HEAD_DIM
    iw = IDX_HEADS * IDX_DIM
    q1, k1t, k1b, v1t, v1tb, qi, kiw, kit, kib = _norm_proj(
        x3, p["norm_mix"][1], wb["w_in1"], tabs, _in1_segs(qw, kvw, iw, False), tm, t // tm)
    r3 = lambda a: a.reshape(nb, t, a.shape[1])
    o1 = _dsa_seq(r3(qi), r3(kiw), r3(kib), r3(q1), r3(k1b), v1tb, tq=DSA_Q_TILE, tk=DSA_K_TILE)
    y, st1 = _ffn_seq(x3, [o1.reshape(n, qw)], [wb["w_out1"]], p["norm_ffn"][1], wb["wg"][1], wb["wu"][1],
                      p["ffn_conv_w"][1], p["ffn_conv_b"][1], wb["wd"][1], p["norm_final"], True, t, tm)

    heads = w // HEAD_DIM
    ffn_state = jnp.stack([st0[:, SUBLANES - 2:], st1[:, SUBLANES - 2:]])
    per_token = lambda a, h: a.reshape(nb, h, HEAD_DIM, t).transpose(0, 3, 1, 2)
    return (y.reshape(nb, t, d), per_token(k0t, heads), per_token(v0t, heads), h_last, ug3[:, t - 3:, :w],
            per_token(k1t, DSA_KV_HEADS), per_token(v1t, DSA_KV_HEADS), kit.transpose(0, 2, 1), ffn_state)


def _sample_group(x, p, wb, cache_k0, cache_v0, state_h0, state_conv0, cache_k1, cache_v1, cache_kidx1,
                  state_ffn, page_table):
    nb, t, d = x.shape
    assert t == 1
    w = p["rg_lambda"].shape[0]
    npg = page_table.shape[1]
    assert (npg * PAGE_SIZE) % MOBA_BLOCK == 0
    tabs = _rope_tables(jnp.full((1,), npg * PAGE_SIZE, I32), nb)
    x2 = x.reshape(nb, d)

    ug, q0, k0, v0 = _norm_proj(x2, p["norm_mix"][0], wb["w_in0"], tabs, _in0_segs(w, True), nb, 1)
    y_rg, h_new = _rglru_step(ug, state_conv0, state_h0, p["rg_conv_w"], p["rg_conv_b"], wb["wa"], wb["wx"],
                              p["rg_gate_a_b"], p["rg_gate_x_b"], p["rg_lambda"])
    heads = w // HEAD_DIM
    paged_t = lambda c: c.transpose(0, 2, 3, 1)
    ck0, cv0 = paged_t(cache_k0), paged_t(cache_v0)
    hsplit = lambda a: a.reshape(nb, heads, HEAD_DIM)
    picked = _moba_pick(page_table, hsplit(q0), ck0)
    sel = picked[:, :, :MOBA_TOPK].reshape(nb, heads * MOBA_TOPK)
    o0 = _moba_step(sel, page_table, hsplit(q0), hsplit(k0), hsplit(v0), ck0, cv0)
    x3, g0 = _ffn_step(x2, [y_rg, o0.astype(BF16)], [wb["w_out0a"], wb["w_out0b"]], p["norm_ffn"][0], wb["wg"][0],
                       wb["wu"][0], p["ffn_conv_w"][0], p["ffn_conv_b"][0], wb["wd"][0], p["norm_final"], False,
                       state_ffn[0])

    qw = p["w_out1"].shape[0]
    kvw = DSA_KV_HEADS * HEAD_DIM
    iw = IDX_HEADS * IDX_DIM
    q1, k1, v1, qi, kiw, ki = _norm_proj(
        x3, p["norm_mix"][1], wb["w_in1"], tabs, _in1_segs(qw, kvw, iw, True), nb, 1)
    qi3 = qi.reshape(nb, IDX_HEADS, IDX_DIM)
    wi3 = kiw[:, IDX_DIM:IDX_DIM + IDX_HEADS].reshape(nb, IDX_HEADS, 1)
    scores = _dsa_score(page_table, qi3, wi3, cache_kidx1.transpose(0, 2, 1))
    bias, bias_new = _dsa_pick(scores, qi3, wi3, ki)
    o1 = _dsa_step(page_table, bias, bias_new, q1.reshape(nb, qw // HEAD_DIM, HEAD_DIM), k1, v1,
                   paged_t(cache_k1), paged_t(cache_v1))
    y, g1 = _ffn_step(x3, [o1.reshape(nb, qw).astype(BF16)], [wb["w_out1"]], p["norm_ffn"][1], wb["wg"][1],
                      wb["wu"][1], p["ffn_conv_w"][1], p["ffn_conv_b"][1], wb["wd"][1], p["norm_final"], True,
                      state_ffn[1])

    conv_new = jnp.concatenate([state_conv0[:, 1:], ug[:, None, :w]], axis=1)
    ffn_state = jnp.stack([jnp.stack([state_ffn[0][:, 1], g0], axis=1), jnp.stack([state_ffn[1][:, 1], g1], axis=1)])
    return (y.reshape(nb, 1, d), k0.reshape(nb, 1, heads, HEAD_DIM), v0.reshape(nb, 1, heads, HEAD_DIM), h_new,
            conv_new, k1.reshape(nb, 1, DSA_KV_HEADS, HEAD_DIM), v1.reshape(nb, 1, DSA_KV_HEADS, HEAD_DIM),
            ki.reshape(nb, 1, IDX_DIM), ffn_state)


def kernel(x_prompt, x_sample, cache_k0, cache_v0, state_h0, state_conv0, cache_k1, cache_v1, cache_kidx1,
           state_ffn, page_table, norm_mix, norm_ffn, norm_final, w_in0, rg_conv_w, rg_conv_b, rg_gate_a_w,
           rg_gate_a_b, rg_gate_x_w, rg_gate_x_b, rg_lambda, w_out0, w_in1, w_out1, ffn_w_gate, ffn_w_up,
           ffn_conv_w, ffn_conv_b, ffn_w_down):
    p = dict(norm_mix=norm_mix, norm_ffn=norm_ffn, norm_final=norm_final, w_in0=w_in0, rg_conv_w=rg_conv_w,
             rg_conv_b=rg_conv_b, rg_gate_a_w=rg_gate_a_w, rg_gate_a_b=rg_gate_a_b, rg_gate_x_w=rg_gate_x_w,
             rg_gate_x_b=rg_gate_x_b, rg_lambda=rg_lambda, w_out0=w_out0, w_in1=w_in1, w_out1=w_out1,
             ffn_w_gate=ffn_w_gate, ffn_w_up=ffn_w_up, ffn_conv_w=ffn_conv_w, ffn_conv_b=ffn_conv_b,
             ffn_w_down=ffn_w_down)
    wb = _prep_weights(p)
    (y_p, k0_p, v0_p, h0_p, conv0_p, k1_p, v1_p, kidx1_p, ffn_p) = _prompt_group(x_prompt, p, wb)
    (y_s, k0_s, v0_s, h0_s, conv0_s, k1_s, v1_s, kidx1_s, ffn_s) = _sample_group(
        x_sample, p, wb, cache_k0, cache_v0, state_h0, state_conv0, cache_k1, cache_v1, cache_kidx1, state_ffn,
        page_table)
    return (y_p, y_s, k0_p, v0_p, h0_p, conv0_p, k1_p, v1_p, kidx1_p, ffn_p,
            k0_s, v0_s, h0_s, conv0_s, k1_s, v1_s, kidx1_s, ffn_s)
```
